```python
import math, functools
import jax, jax.numpy as jnp
from jax import lax
import numpy as np

D_MODEL = 1024
BATCH = 4
SEQ = 4096
DEPTH = 2
DEC_BATCH = 32
DEC_SEQ = 4
PAST_LEN = 16384
PAGE_SIZE = 128

D_LRU = D_MODEL // 4
LRU_BLOCKS = 4
LRU_BW = D_LRU // LRU_BLOCKS
CONV_W = 4
LRU_C = 8.0
N_HEADS = 8
HEAD_DIM = 64
D_ATT = N_HEADS * HEAD_DIM
MOBA_BLOCK = 256
MOBA_TOPK = 3
Q_CHUNK = 32
ROPE_THETA = 10000.0
D_S5 = D_MODEL // 4
S5_GROUP = 16
S5_GROUPS = D_S5 // S5_GROUP
S5_STATE = 64
D_MIX = D_LRU + D_ATT + D_S5
IN_COLS = 2 * D_LRU + 4 * D_ATT + 2 * D_S5
IN_SPLITS = (D_LRU, 2 * D_LRU, 2 * D_LRU + D_ATT, 2 * D_LRU + 2 * D_ATT,
             2 * D_LRU + 3 * D_ATT, 2 * D_LRU + 4 * D_ATT, 2 * D_LRU + 4 * D_ATT + D_S5)
EPS = 1e-6
POOL_NUM = 5
POOL_DEN = 4

kernel_name = 'hybrid_rglru_moba_s5_decode_step'


def rmsnorm(x, g):
    xf = x.astype(jnp.float32)
    y = xf * lax.rsqrt(jnp.mean(xf * xf, axis=-1, keepdims=True) + EPS)
    return (y * g.astype(jnp.float32)).astype(x.dtype)


def rope(x, pos):
    half = HEAD_DIM // 2
    inv = jnp.power(ROPE_THETA, -jnp.arange(half, dtype=jnp.float32) / half)
    ang = pos.astype(jnp.float32)[:, None] * inv[None, :]
    cos, sin = jnp.cos(ang), jnp.sin(ang)
    x1 = x[..., :half].astype(jnp.float32)
    x2 = x[..., half:].astype(jnp.float32)
    out = jnp.concatenate([x1 * cos - x2 * sin, x2 * cos + x1 * sin], axis=-1)
    return out.astype(x.dtype)


def _lin_combine(e1, e2):
    a1, b1 = e1
    a2, b2 = e2
    return a1 * a2, a2 * b1 + b2


def _clin_combine(e1, e2):
    ar1, ai1, br1, bi1 = e1
    ar2, ai2, br2, bi2 = e2
    return (ar2 * ar1 - ai2 * ai1, ar2 * ai1 + ai2 * ar1,
            ar2 * br1 - ai2 * bi1 + br2, ar2 * bi1 + ai2 * br1 + bi2)


def rg_lru(u, h0, buf0, cw, cb, wr, br, wi, bi, lam):
    bn, s, _ = u.shape
    ext = jnp.concatenate([buf0.astype(u.dtype), u], axis=1)
    xc = cb
    for j in range(CONV_W):
        xc = xc + ext[:, j:j + s] * cw[j]
    blk = xc.reshape(bn, s, LRU_BLOCKS, LRU_BW)
    r = jax.nn.sigmoid(jnp.einsum('bsnc,ncd->bsnd', blk, wr).reshape(bn, s, D_LRU) + br)
    i = jax.nn.sigmoid(jnp.einsum('bsnc,ncd->bsnd', blk, wi).reshape(bn, s, D_LRU) + bi)
    log_a = LRU_C * r.astype(jnp.float32) * jax.nn.log_sigmoid(lam.astype(jnp.float32))
    a = jnp.exp(log_a)
    mult = jnp.sqrt(-jnp.expm1(2.0 * log_a))
    b = mult * i.astype(jnp.float32) * xc.astype(jnp.float32)
    b = b.at[:, 0].add(a[:, 0] * h0.astype(jnp.float32))
    _, h = lax.associative_scan(_lin_combine, (a, b), axis=1)
    return h.astype(u.dtype), h[:, -1], ext[:, s:]


def s5_mixer(u, s_re0, s_im0, a_re, a_im, log_dt, b_re, b_im, c_re, c_im, d, w_glu, b_glu):
    bn, s, _ = u.shape
    uf = u.astype(jnp.float32).reshape(bn, s, S5_GROUPS, S5_GROUP)
    dt = jnp.exp(log_dt.astype(jnp.float32))[:, None]
    ar = a_re.astype(jnp.float32)
    ai = a_im.astype(jnp.float32)
    mag = jnp.exp(ar * dt)
    lb_re = mag * jnp.cos(ai * dt)
    lb_im = mag * jnp.sin(ai * dt)
    den = ar * ar + ai * ai
    n_re = lb_re - 1.0
    co_re = (n_re * ar + lb_im * ai) / den
    co_im = (lb_im * ar - n_re * ai) / den
    bre = b_re.astype(jnp.float32)
    bim = b_im.astype(jnp.float32)
    bb_re = co_re[..., None] * bre - co_im[..., None] * bim
    bb_im = co_re[..., None] * bim + co_im[..., None] * bre
    bu_re = jnp.einsum('bsgc,gpc->bsgp', uf, bb_re)
    bu_im = jnp.einsum('bsgc,gpc->bsgp', uf, bb_im)
    sr = s_re0.astype(jnp.float32)
    si = s_im0.astype(jnp.float32)
    bu_re = bu_re.at[:, 0].add(lb_re * sr - lb_im * si)
    bu_im = bu_im.at[:, 0].add(lb_re * si + lb_im * sr)
    aa_re = jnp.broadcast_to(lb_re, bu_re.shape)
    aa_im = jnp.broadcast_to(lb_im, bu_im.shape)
    _, _, xr, xi = lax.associative_scan(_clin_combine, (aa_re, aa_im, bu_re, bu_im), axis=1)
    y = (jnp.einsum('bsgp,gcp->bsgc', xr, c_re.astype(jnp.float32))
         - jnp.einsum('bsgp,gcp->bsgc', xi, c_im.astype(jnp.float32))
         + d.astype(jnp.float32).reshape(S5_GROUPS, S5_GROUP) * uf)
    z = jax.nn.gelu(y.reshape(bn, s, D_S5))
    out = z * jax.nn.sigmoid(z @ w_glu.astype(jnp.float32) + b_glu.astype(jnp.float32))
    return out.astype(u.dtype), xr[:, -1], xi[:, -1]


def moba_attend(q, k_sel, v_sel, sel_mask, k_own, v_own, own_mask):
    qf = q.astype(jnp.float32) * (HEAD_DIM ** -0.5)
    s_own = jnp.einsum('bhqd,bhld->bhql', qf, k_own.astype(jnp.float32))
    s_own = jnp.where(own_mask, s_own, -jnp.inf)
    if k_sel is None:
        p = jax.nn.softmax(s_own, axis=-1)
        return jnp.einsum('bhql,bhld->bhqd', p, v_own.astype(jnp.float32)).astype(q.dtype)
    s_sel = jnp.einsum('bhqd,bhqkd->bhqk', qf, k_sel.astype(jnp.float32))
    if sel_mask is not None:
        s_sel = jnp.where(sel_mask, s_sel, -jnp.inf)
    n_sel = s_sel.shape[-1]
    p = jax.nn.softmax(jnp.concatenate([s_sel, s_own], axis=-1), axis=-1)
    out = (jnp.einsum('bhqk,bhqkd->bhqd', p[..., :n_sel], v_sel.astype(jnp.float32))
           + jnp.einsum('bhql,bhld->bhqd', p[..., n_sel:], v_own.astype(jnp.float32)))
    return out.astype(q.dtype)


def moba_prompt(q, k, v):
    bn, nh, s, hd = q.shape
    nb = -(-s // MOBA_BLOCK)
    pad = nb * MOBA_BLOCK - s
    kp = jnp.pad(k, ((0, 0), (0, 0), (0, pad), (0, 0)))
    vp = jnp.pad(v, ((0, 0), (0, 0), (0, pad), (0, 0)))
    kb = kp.reshape(bn, nh, nb, MOBA_BLOCK, hd)
    vb = vp.reshape(bn, nh, nb, MOBA_BLOCK, hd)
    k_mean = jnp.mean(kb.astype(jnp.float32), axis=3)
    topk = min(MOBA_TOPK, nb)
    bi = jnp.arange(bn)[:, None, None, None]
    hi = jnp.arange(nh)[None, :, None, None]
    blk_ids = jnp.arange(nb)

    def one_chunk(n):
        c0 = n * Q_CHUNK
        qc = lax.dynamic_slice_in_dim(q, c0, Q_CHUNK, axis=2)
        q_pos = c0 + jnp.arange(Q_CHUNK)
        own = c0 // MOBA_BLOCK
        past = blk_ids < own
        s_blk = jnp.einsum('bhqd,bhnd->bhqn', qc.astype(jnp.float32), k_mean)
        s_blk = jnp.where(past, s_blk, -jnp.inf)
        _, idx = lax.top_k(s_blk, topk)
        valid = past[idx]
        k_sel = kb[bi, hi, idx].reshape(bn, nh, Q_CHUNK, topk * MOBA_BLOCK, hd)
        v_sel = vb[bi, hi, idx].reshape(bn, nh, Q_CHUNK, topk * MOBA_BLOCK, hd)
        sel_mask = jnp.repeat(valid, MOBA_BLOCK, axis=-1)
        start = own * MOBA_BLOCK
        k_own = lax.dynamic_slice_in_dim(kp, start, MOBA_BLOCK, axis=2)
        v_own = lax.dynamic_slice_in_dim(vp, start, MOBA_BLOCK, axis=2)
        own_mask = (start + jnp.arange(MOBA_BLOCK))[None, :] <= q_pos[:, None]
        return moba_attend(qc, k_sel, v_sel, sel_mask, k_own, v_own, own_mask)

    out = lax.map(one_chunk, jnp.arange(s // Q_CHUNK))
    return out.transpose(1, 2, 0, 3, 4).reshape(bn, nh, s, hd)


def moba_sample(q, k, v, cache_k, cache_v, page_table, layer):
    bn, nh, t, hd = q.shape
    ppb = MOBA_BLOCK // PAGE_SIZE
    n_pages = PAST_LEN // PAGE_SIZE
    n_full = PAST_LEN // MOBA_BLOCK
    r = (n_pages - n_full * ppb) * PAGE_SIZE
    own_pages = page_table[:, n_full * ppb:]
    k_past = cache_k[layer, own_pages].transpose(0, 2, 1, 3, 4).reshape(bn, nh, r, hd)
    v_past = cache_v[layer, own_pages].transpose(0, 2, 1, 3, 4).reshape(bn, nh, r, hd)
    k_own = jnp.concatenate([k_past.astype(k.dtype), k], axis=2)
    v_own = jnp.concatenate([v_past.astype(v.dtype), v], axis=2)
    own_mask = jnp.arange(r + t)[None, :] <= (r + jnp.arange(t))[:, None]
    if n_full == 0:
        return moba_attend(q, None, None, None, k_own, v_own, own_mask)
    full_pages = page_table[:, :n_full * ppb]
    k_page_sum = jnp.sum(cache_k[layer, full_pages].astype(jnp.float32), axis=3)
    k_mean = k_page_sum.reshape(bn, n_full, ppb, nh, hd).sum(axis=2).transpose(0, 2, 1, 3) / MOBA_BLOCK
    s_blk = jnp.einsum('bhqd,bhnd->bhqn', q.astype(jnp.float32), k_mean)
    _, idx = lax.top_k(s_blk, min(MOBA_TOPK, n_full))
    bi = jnp.arange(bn)[:, None, None, None, None]
    hi = jnp.arange(nh)[None, :, None, None, None]
    pids = page_table[bi, idx[..., None] * ppb + jnp.arange(ppb)]
    k_sel = cache_k[layer, pids, hi].reshape(bn, nh, t, -1, hd)
    v_sel = cache_v[layer, pids, hi].reshape(bn, nh, t, -1, hd)
    return moba_attend(q, k_sel, v_sel, None, k_own, v_own, own_mask)


def mixer_layer(x, c, pos, lru_h0, lru_buf0, s5_re0, s5_im0, attend, lw):
    (g, w_ada, b_ada, w_in, w_out, cw, cb, wr, br, wi, bi, lam,
     a_re, a_im, log_dt, b_re, b_im, c_re, c_im, d, w_glu, b_glu) = lw
    bn, s, _ = x.shape
    mod = jax.nn.silu(c) @ w_ada + b_ada
    shift, scale, gate = jnp.split(mod, 3, axis=-1)
    h = rmsnorm(x, g) * (1.0 + scale[:, None]) + shift[:, None]
    proj = h @ w_in
    a_x, a_g, q, k, v, b_g, s_x, s_g = jnp.split(proj, IN_SPLITS, axis=-1)
    a_out, lru_h, lru_buf = rg_lru(a_x, lru_h0, lru_buf0, cw, cb, wr, br, wi, bi, lam)
    to_heads = lambda t_: t_.reshape(bn, s, N_HEADS, HEAD_DIM).transpose(0, 2, 1, 3)
    qh = rope(to_heads(q), pos)
    kh = rope(to_heads(k), pos)
    vh = to_heads(v)
    att = attend(qh, kh, vh).transpose(0, 2, 1, 3).reshape(bn, s, D_ATT)
    s_out, s5_re, s5_im = s5_mixer(s_x, s5_re0, s5_im0, a_re, a_im, log_dt, b_re, b_im, c_re, c_im, d, w_glu, b_glu)
    mixed = jnp.concatenate([a_out * jax.nn.silu(a_g), att * jax.nn.silu(b_g), s_out * jax.nn.silu(s_g)], axis=-1)
    x = x + gate[:, None] * (mixed @ w_out)
    return x, (kh, vh, lru_h, lru_buf, s5_re, s5_im)


def setup_inputs(seed: int = 0) -> dict:
    key = jax.random.key(seed)
    keys = iter(jax.random.split(key, 40))
    nrm = lambda shape, sc: sc * jax.random.normal(next(keys), shape, jnp.float32)
    n_pages = PAST_LEN // PAGE_SIZE
    n_pool = (DEC_BATCH * n_pages * POOL_NUM) // POOL_DEN
    page_table = jax.random.permutation(next(keys), n_pool)[:DEC_BATCH * n_pages]
    page_table = page_table.reshape(DEC_BATCH, n_pages).astype(jnp.int32)
    u = jax.random.uniform(next(keys), (DEPTH, D_LRU), jnp.float32, 0.9, 0.999)
    sa = u ** (1.0 / LRU_C)
    lru_lam = jnp.log(sa) - jnp.log1p(-sa)
    a_im = np.pi * jnp.arange(S5_STATE, dtype=jnp.float32)[None, None, :] + nrm((DEPTH, S5_GROUPS, S5_STATE), 0.01)
    log_dt = jax.random.uniform(next(keys), (DEPTH, S5_GROUPS), jnp.float32, math.log(1e-3), math.log(1e-1))
    return {
        'x_prompt': nrm((BATCH, SEQ, D_MODEL), 1.0),
        'x_sample': nrm((DEC_BATCH, DEC_SEQ, D_MODEL), 1.0),
        'cache_k': nrm((DEPTH, n_pool, N_HEADS, PAGE_SIZE, HEAD_DIM), 1.0),
        'cache_v': nrm((DEPTH, n_pool, N_HEADS, PAGE_SIZE, HEAD_DIM), 1.0),
        'state_lru_h': nrm((DEPTH, DEC_BATCH, D_LRU), 0.5),
        'state_lru_conv': nrm((DEPTH, DEC_BATCH, CONV_W - 1, D_LRU), 1.0),
        'state_s5_re': nrm((DEPTH, DEC_BATCH, S5_GROUPS, S5_STATE), 0.3),
        'state_s5_im': nrm((DEPTH, DEC_BATCH, S5_GROUPS, S5_STATE), 0.3),
        'page_table': page_table,
        'c_prompt': nrm((BATCH, D_MODEL), 1.0),
        'c_sample': nrm((DEC_BATCH, D_MODEL), 1.0),
        'norm_g': 1.0 + nrm((DEPTH, D_MODEL), 0.05),
        'w_ada': nrm((DEPTH, D_MODEL, 3 * D_MODEL), 0.2 * D_MODEL ** -0.5),
        'b_ada': nrm((DEPTH, 3 * D_MODEL), 0.02),
        'w_in': nrm((DEPTH, D_MODEL, IN_COLS), D_MODEL ** -0.5),
        'w_out': nrm((DEPTH, D_MIX, D_MODEL), D_MIX ** -0.5),
        'lru_conv_w': nrm((DEPTH, CONV_W, D_LRU), CONV_W ** -0.5),
        'lru_conv_b': nrm((DEPTH, D_LRU), 0.02),
        'lru_w_r': nrm((DEPTH, LRU_BLOCKS, LRU_BW, LRU_BW), LRU_BW ** -0.5),
        'lru_b_r': nrm((DEPTH, D_LRU), 0.02),
        'lru_w_i': nrm((DEPTH, LRU_BLOCKS, LRU_BW, LRU_BW), LRU_BW ** -0.5),
        'lru_b_i': nrm((DEPTH, D_LRU), 0.02),
        'lru_lam': lru_lam,
        's5_a_re': -0.5 + nrm((DEPTH, S5_GROUPS, S5_STATE), 0.01),
        's5_a_im': a_im,
        's5_log_dt': log_dt,
        's5_b_re': nrm((DEPTH, S5_GROUPS, S5_STATE, S5_GROUP), (2 * S5_GROUP) ** -0.5),
        's5_b_im': nrm((DEPTH, S5_GROUPS, S5_STATE, S5_GROUP), (2 * S5_GROUP) ** -0.5),
        's5_c_re': nrm((DEPTH, S5_GROUPS, S5_GROUP, S5_STATE), S5_STATE ** -0.5),
        's5_c_im': nrm((DEPTH, S5_GROUPS, S5_GROUP, S5_STATE), S5_STATE ** -0.5),
        's5_d': nrm((DEPTH, D_S5), 0.5),
        's5_w_glu': nrm((DEPTH, D_S5, D_S5), D_S5 ** -0.5),
        's5_b_glu': nrm((DEPTH, D_S5), 0.02),
        'final_g': 1.0 + nrm((D_MODEL,), 0.05),
    }


def reference(x_prompt, x_sample, cache_k, cache_v, state_lru_h, state_lru_conv, state_s5_re, state_s5_im,
              page_table, c_prompt, c_sample, norm_g, w_ada, b_ada, w_in, w_out, lru_conv_w, lru_conv_b,
              lru_w_r, lru_b_r, lru_w_i, lru_b_i, lru_lam, s5_a_re, s5_a_im, s5_log_dt, s5_b_re, s5_b_im,
              s5_c_re, s5_c_im, s5_d, s5_w_glu, s5_b_glu, final_g):
    pos_p = jnp.arange(SEQ, dtype=jnp.int32)
    pos_s = PAST_LEN + jnp.arange(DEC_SEQ, dtype=jnp.int32)
    xp, xs = x_prompt, x_sample
    zh = jnp.zeros((BATCH, D_LRU), x_prompt.dtype)
    zbuf = jnp.zeros((BATCH, CONV_W - 1, D_LRU), x_prompt.dtype)
    zs = jnp.zeros((BATCH, S5_GROUPS, S5_STATE), jnp.float32)
    kp_l, vp_l, hp_l, bp_l, rp_l, ip_l = [], [], [], [], [], []
    ks_l, vs_l, hs_l, bs_l, rs_l, is_l = [], [], [], [], [], []
    for l in range(DEPTH):
        lw = (norm_g[l], w_ada[l], b_ada[l], w_in[l], w_out[l], lru_conv_w[l], lru_conv_b[l],
              lru_w_r[l], lru_b_r[l], lru_w_i[l], lru_b_i[l], lru_lam[l], s5_a_re[l], s5_a_im[l],
              s5_log_dt[l], s5_b_re[l], s5_b_im[l], s5_c_re[l], s5_c_im[l], s5_d[l], s5_w_glu[l], s5_b_glu[l])
        xp, (kp, vp, hp, bp, rp, ip) = mixer_layer(xp, c_prompt, pos_p, zh, zbuf, zs, zs, moba_prompt, lw)
        attend_s = functools.partial(moba_sample, cache_k=cache_k, cache_v=cache_v,
                                     page_table=page_table, layer=l)
        xs, (ks, vs, hs, bs, rs, is_) = mixer_layer(xs, c_sample, pos_s, state_lru_h[l], state_lru_conv[l],
                                                    state_s5_re[l], state_s5_im[l], attend_s, lw)
        kp_l.append(kp); vp_l.append(vp); hp_l.append(hp); bp_l.append(bp); rp_l.append(rp); ip_l.append(ip)
        ks_l.append(ks); vs_l.append(vs); hs_l.append(hs); bs_l.append(bs); rs_l.append(rs); is_l.append(is_)
    y_prompt = rmsnorm(xp, final_g)
    y_sample = rmsnorm(xs, final_g)
    return (y_prompt, y_sample,
            jnp.stack(kp_l), jnp.stack(vp_l), jnp.stack(hp_l), jnp.stack(bp_l), jnp.stack(rp_l), jnp.stack(ip_l),
            jnp.stack(ks_l), jnp.stack(vs_l), jnp.stack(hs_l), jnp.stack(bs_l), jnp.stack(rs_l), jnp.stack(is_l))
```

```python
import functools
import math

import jax
import jax.numpy as jnp
from jax import lax
from jax.experimental import pallas as pl
from jax.experimental.pallas import tpu as pltpu

F32 = jnp.float32
BF16 = jnp.bfloat16
HIGHEST = lax.Precision.HIGHEST

D_MODEL = 1024
DEPTH = 2
PAST_LEN = 16384
PAGE_SIZE = 128
D_LRU = 256
LRU_BLOCKS = 4
LRU_BW = D_LRU // LRU_BLOCKS
CONV_W = 4
LRU_C = 8.0
N_HEADS = 8
HEAD_DIM = 64
HALF = HEAD_DIM // 2
D_ATT = N_HEADS * HEAD_DIM
MOBA_BLOCK = 256
MOBA_TOPK = 3
ROPE_THETA = 10000.0
D_S5 = 256
S5_GROUP = 16
S5_GROUPS = D_S5 // S5_GROUP
S5_STATE = 64
S5_P = S5_GROUPS * S5_STATE
D_MIX = D_LRU + D_ATT + D_S5
IN_COLS = 2 * D_LRU + 4 * D_ATT + 2 * D_S5
EPS = 1e-6

LANES = 128
NEG = -1e30

U_COLS = D_LRU + D_S5
G_COLS = D_MIX
OFF_U = 0
OFF_G = OFF_U + U_COLS
OFF_Q = OFF_G + G_COLS
OFF_K = OFF_Q + D_ATT
OFF_V = OFF_K + D_ATT

VMEM_LIMIT = 56 * 1024 * 1024


def _cparams(sem):
    return pltpu.CompilerParams(dimension_semantics=sem, vmem_limit_bytes=VMEM_LIMIT)


def _silu(x):
    return x * jax.nn.sigmoid(x)


def _mod_kernel(c_ref, w_ref, b_ref, o_ref):
    o_ref[0] = jnp.dot(_silu(c_ref[...]), w_ref[0], precision=HIGHEST,
                       preferred_element_type=F32) + b_ref[0]


def _modulation(c_all, w_ada, b_ada):
    n = c_all.shape[0]
    tn = D_MODEL
    return pl.pallas_call(
        _mod_kernel,
        grid=(DEPTH, 3 * D_MODEL // tn),
        in_specs=[pl.BlockSpec((n, D_MODEL), lambda l, j: (0, 0)),
                  pl.BlockSpec((1, D_MODEL, tn), lambda l, j: (l, 0, j)),
                  pl.BlockSpec((1, 1, tn), lambda l, j: (l, 0, j))],
        out_specs=pl.BlockSpec((1, n, tn), lambda l, j: (l, 0, j)),
        out_shape=jax.ShapeDtypeStruct((DEPTH, n, 3 * D_MODEL), F32),
        compiler_params=_cparams(("arbitrary", "arbitrary")),
        name="modulation",
    )(c_all, w_ada, b_ada.reshape(DEPTH, 1, 3 * D_MODEL))


def _inproj_kernel(x_ref, sc_ref, sh_ref, g_ref, w_ref, cos_ref, sin_ref,
                   u_ref, gate_ref, q_ref, k_ref, v_ref):
    x = x_ref[0]
    h = x * lax.rsqrt(jnp.mean(x * x, axis=-1, keepdims=True) + EPS) * g_ref[...]
    h = h * (1.0 + sc_ref[0]) + sh_ref[0]
    hb = h.astype(BF16)

    def proj(lo, n):
        return jnp.dot(hb, w_ref[:, lo:lo + n], preferred_element_type=F32)

    u_ref[0] = proj(OFF_U, U_COLS)
    gate_ref[0] = _silu(proj(OFF_G, G_COLS))
    cos = cos_ref[...]
    sin = sin_ref[...]
    lane = lax.broadcasted_iota(jnp.int32, cos.shape, 1)
    first_half = (lane & (HEAD_DIM - 1)) < HALF
    for off, ref in ((OFF_Q, q_ref), (OFF_K, k_ref)):
        p = proj(off, D_ATT)
        for j in range(D_ATT // LANES):
            xs = p[:, j * LANES:(j + 1) * LANES]
            partner = jnp.where(first_half, pltpu.roll(xs, LANES - HALF, 1), pltpu.roll(xs, HALF, 1))
            r = xs * cos + partner * sin
            ref[0, 2 * j] = r[:, :HEAD_DIM]
            ref[0, 2 * j + 1] = r[:, HEAD_DIM:]
    p = proj(OFF_V, D_ATT)
    for hh in range(N_HEADS):
        v_ref[0, hh] = p[:, hh * HEAD_DIM:(hh + 1) * HEAD_DIM]


def _inproj(x, scale, shift, g, w_perm, cos_t, sin_t, tm):
    nb, s, _ = x.shape
    rm = scale.shape[1]
    mod_block = (1, tm, D_MODEL) if rm == s else (1, 1, D_MODEL)
    mod_map = (lambda b, i: (b, i, 0)) if rm == s else (lambda b, i: (b, 0, 0))
    head_spec = pl.BlockSpec((1, N_HEADS, tm, HEAD_DIM), lambda b, i: (b, 0, i, 0))
    head_shape = jax.ShapeDtypeStruct((nb, N_HEADS, s, HEAD_DIM), F32)
    return pl.pallas_call(
        _inproj_kernel,
        grid=(nb, s // tm),
        in_specs=[pl.BlockSpec((1, tm, D_MODEL), lambda b, i: (b, i, 0)),
                  pl.BlockSpec(mod_block, mod_map),
                  pl.BlockSpec(mod_block, mod_map),
                  pl.BlockSpec((1, D_MODEL), lambda b, i: (0, 0)),
                  pl.BlockSpec((D_MODEL, IN_COLS), lambda b, i: (0, 0)),
                  pl.BlockSpec((tm, LANES), lambda b, i: (i, 0)),
                  pl.BlockSpec((tm, LANES), lambda b, i: (i, 0))],
        out_specs=[pl.BlockSpec((1, tm, U_COLS), lambda b, i: (b, i, 0)),
                   pl.BlockSpec((1, tm, G_COLS), lambda b, i: (b, i, 0)),
                   head_spec, head_spec, head_spec],
        out_shape=[jax.ShapeDtypeStruct((nb, s, U_COLS), F32),
                   jax.ShapeDtypeStruct((nb, s, G_COLS), F32),
                   head_shape, head_shape, head_shape],
        compiler_params=_cparams(("arbitrary", "arbitrary")),
        name="inproj",
    )(x, scale, shift, g.reshape(1, D_MODEL), w_perm, cos_t, sin_t)


def _outproj_kernel(x_ref, as_ref, att_ref, gate_ref, gm_ref, w_ref, fg_ref, o_ref, *, final):
    g = gate_ref[0]
    a_s = as_ref[0]
    mixed = jnp.concatenate([a_s[:, :D_LRU] * g[:, :D_LRU],
                             att_ref[0] * g[:, D_LRU:D_LRU + D_ATT],
                             a_s[:, D_LRU:] * g[:, D_LRU + D_ATT:]], axis=1).astype(BF16)
    y = jnp.dot(mixed, w_ref[...], preferred_element_type=F32)
    xn = x_ref[0] + gm_ref[0] * y
    if final:
        xn = xn * lax.rsqrt(jnp.mean(xn * xn, axis=-1, keepdims=True) + EPS) * fg_ref[...]
    o_ref[0] = xn


def _outproj(x, as_out, att, gates, gate_mod, w_out, final_g, tm, final):
    nb, s, _ = x.shape
    rm = gate_mod.shape[1]
    mod_block = (1, tm, D_MODEL) if rm == s else (1, 1, D_MODEL)
    mod_map = (lambda b, i: (b, i, 0)) if rm == s else (lambda b, i: (b, 0, 0))
    row = lambda n: pl.BlockSpec((1, tm, n), lambda b, i: (b, i, 0))
    return pl.pallas_call(
        functools.partial(_outproj_kernel, final=final),
        grid=(nb, s // tm),
        in_specs=[row(D_MODEL), row(U_COLS), row(D_ATT), row(G_COLS),
                  pl.BlockSpec(mod_block, mod_map),
                  pl.BlockSpec((D_MIX, D_MODEL), lambda b, i: (0, 0)),
                  pl.BlockSpec((1, D_MODEL), lambda b, i: (0, 0))],
        out_specs=row(D_MODEL),
        out_shape=jax.ShapeDtypeStruct((nb, s, D_MODEL), F32),
        compiler_params=_cparams(("arbitrary", "arbitrary")),
        name="outproj",
    )(x, as_out, att, gates, gate_mod, w_out, final_g.reshape(1, D_MODEL))


def _lru_gates(xc, wg_ref, bg_ref, c8_ref):
    gates = jnp.dot(xc.astype(BF16), wg_ref[...], preferred_element_type=F32) + bg_ref[...]
    r = jax.nn.sigmoid(gates[:, :D_LRU])
    i = jax.nn.sigmoid(gates[:, D_LRU:])
    log_a = r * c8_ref[...]
    a = jnp.exp(log_a)
    mult = jnp.sqrt(1.0 - jnp.exp(2.0 * log_a))
    return a, mult * i * xc


def _s5_out(x_cat_bf16, sx, cbd_ref, d_ref, wglu_ref, bglu_ref):
    y = jnp.dot(x_cat_bf16, cbd_ref[...], preferred_element_type=F32) + d_ref[...] * sx
    z = jax.nn.gelu(y)
    return z * jax.nn.sigmoid(jnp.dot(z.astype(BF16), wglu_ref[...], preferred_element_type=F32)
                              + bglu_ref[...])


def _rec_prompt_kernel(u_ref, cw_ref, cb_ref, wg_ref, bg_ref, c8_ref, lbr_ref, lbi_ref,
                       bbd_ref, cbd_ref, d_ref, wglu_ref, bglu_ref,
                       o_ref, hl_ref, cbuf_ref, sre_ref, sim_ref,
                       ext, a_s, b_s, h_s, bu_s, x_s, h_st, x_st, *, nb, tc, pitch):
    c = pl.program_id(0)
    nl = D_LRU // LANES
    ns = S5_P // LANES

    @pl.when(c == 0)
    def _():
        ext[:, 0:8, :] = jnp.zeros((nb, 8, D_LRU), F32)
        h_st[...] = jnp.zeros(h_st.shape, F32)
        x_st[...] = jnp.zeros(x_st.shape, F32)

    for b in range(nb):
        r0 = b * pitch
        ext[b, 8:8 + tc, :] = u_ref[b, :, 0:D_LRU]
        xc = cb_ref[...]
        for j in range(CONV_W):
            xc = xc + ext[b, 8 - (CONV_W - 1) + j:8 - (CONV_W - 1) + j + tc, :] * cw_ref[j:j + 1, :]
        ext[b, 8 - (CONV_W - 1):8, :] = ext[b, 8 + tc - (CONV_W - 1):8 + tc, :]
        a, bb = _lru_gates(xc, wg_ref, bg_ref, c8_ref)
        for j in range(nl):
            a_s[j, r0:r0 + tc, :] = a[:, j * LANES:(j + 1) * LANES]
            b_s[j, r0:r0 + tc, :] = bb[:, j * LANES:(j + 1) * LANES]
        sxb = u_ref[b, :, D_LRU:U_COLS].astype(BF16)
        for j in range(ns):
            bu = jnp.dot(sxb, bbd_ref[:, 2 * j * LANES:(2 * j + 2) * LANES], preferred_element_type=F32)
            bu_s[j, r0:r0 + tc, :] = bu[:, :LANES]
            bu_s[ns + j, r0:r0 + tc, :] = bu[:, LANES:]

    def step(t, carry):
        h, xr, xi = carry
        rows = pl.ds(t, nb, stride=pitch)
        h_new, xr_new, xi_new = [], [], []
        for j in range(nl):
            hj = a_s[j, rows, :] * h[j] + b_s[j, rows, :]
            h_s[j, rows, :] = hj
            h_new.append(hj)
        for j in range(ns):
            lr = lbr_ref[j]
            li = lbi_ref[j]
            nr = lr * xr[j] - li * xi[j] + bu_s[j, rows, :]
            ni = lr * xi[j] + li * xr[j] + bu_s[ns + j, rows, :]
            x_s[j, rows, :] = nr
            x_s[ns + j, rows, :] = ni
            xr_new.append(nr)
            xi_new.append(ni)
        return tuple(h_new), tuple(xr_new), tuple(xi_new)

    init = (tuple(h_st[j] for j in range(nl)),
            tuple(x_st[j] for j in range(ns)),
            tuple(x_st[ns + j] for j in range(ns)))
    h, xr, xi = lax.fori_loop(0, tc, step, init)
    for j in range(nl):
        h_st[j] = h[j]
        hl_ref[:, j * LANES:(j + 1) * LANES] = h[j]
    for j in range(ns):
        x_st[j] = xr[j]
        x_st[ns + j] = xi[j]
        sre_ref[:, j * LANES:(j + 1) * LANES] = xr[j]
        sim_ref[:, j * LANES:(j + 1) * LANES] = xi[j]

    for b in range(nb):
        r0 = b * pitch
        for j in range(nl):
            o_ref[b, :, j * LANES:(j + 1) * LANES] = h_s[j, r0:r0 + tc, :]
        x_cat = jnp.concatenate([x_s[j, r0:r0 + tc, :].astype(BF16) for j in range(2 * ns)], axis=1)
        o_ref[b, :, D_LRU:U_COLS] = _s5_out(x_cat, u_ref[b, :, D_LRU:U_COLS], cbd_ref, d_ref,
                                            wglu_ref, bglu_ref)
        cbuf_ref[b] = ext[b, 8 - (CONV_W - 1):8, :]


def _rec_prompt(u, pr, tc):
    nb, s, _ = u.shape
    pitch = tc + 8
    nl = D_LRU // LANES
    ns = S5_P // LANES
    full = lambda a: pl.BlockSpec(a.shape, lambda c, _n=a.ndim: (0,) * _n)
    params = [pr["cw"], pr["cb"], pr["wg"], pr["bg"], pr["c8"], pr["lbr"], pr["lbi"],
              pr["bbd"], pr["cbd"], pr["d"], pr["wglu"], pr["bglu"]]
    st = lambda *shape: pl.BlockSpec(shape, lambda c, _n=len(shape): (0,) * _n)
    return pl.pallas_call(
        functools.partial(_rec_prompt_kernel, nb=nb, tc=tc, pitch=pitch),
        grid=(s // tc,),
        in_specs=[pl.BlockSpec((nb, tc, U_COLS), lambda c: (0, c, 0))] + [full(a) for a in params],
        out_specs=[pl.BlockSpec((nb, tc, U_COLS), lambda c: (0, c, 0)),
                   st(nb, D_LRU), st(nb, CONV_W - 1, D_LRU), st(nb, S5_P), st(nb, S5_P)],
        out_shape=[jax.ShapeDtypeStruct((nb, s, U_COLS), F32),
                   jax.ShapeDtypeStruct((nb, D_LRU), F32),
                   jax.ShapeDtypeStruct((nb, CONV_W - 1, D_LRU), F32),
                   jax.ShapeDtypeStruct((nb, S5_P), F32),
                   jax.ShapeDtypeStruct((nb, S5_P), F32)],
        scratch_shapes=[pltpu.VMEM((nb, tc + 8, D_LRU), F32),
                        pltpu.VMEM((nl, nb * pitch, LANES), F32),
                        pltpu.VMEM((nl, nb * pitch, LANES), F32),
                        pltpu.VMEM((nl, nb * pitch, LANES), F32),
                        pltpu.VMEM((2 * ns, nb * pitch, LANES), F32),
                        pltpu.VMEM((2 * ns, nb * pitch, LANES), F32),
                        pltpu.VMEM((nl, nb, LANES), F32),
                        pltpu.VMEM((2 * ns, nb, LANES), F32)],
        compiler_params=_cparams(("arbitrary",)),
        name="rec_prompt",
    )(u, *params)


def _rec_sample_kernel(u_ref, h0_ref, buf0_ref, s0r_ref, s0i_ref,
                       cw_ref, cb_ref, wg_ref, bg_ref, c8_ref, lbr_ref, lbi_ref,
                       bbd_ref, cbd_ref, d_ref, wglu_ref, bglu_ref,
                       o_ref, hl_ref, cbuf_ref, sre_ref, sim_ref, *, nt):
    ns = S5_P // LANES
    ext = [buf0_ref[j] for j in range(CONV_W - 1)] + [u_ref[t, :, 0:D_LRU] for t in range(nt)]
    xcs = []
    for t in range(nt):
        xc = cb_ref[...]
        for j in range(CONV_W):
            xc = xc + ext[t + j] * cw_ref[j:j + 1, :]
        xcs.append(xc)
    a, bb = _lru_gates(jnp.concatenate(xcs, axis=0), wg_ref, bg_ref, c8_ref)
    nbat = h0_ref.shape[0]
    h = h0_ref[...]
    for t in range(nt):
        h = a[t * nbat:(t + 1) * nbat] * h + bb[t * nbat:(t + 1) * nbat]
        o_ref[t, :, 0:D_LRU] = h
    hl_ref[...] = h
    for j in range(CONV_W - 1):
        cbuf_ref[j] = ext[nt + j]

    sx = jnp.concatenate([u_ref[t, :, D_LRU:U_COLS] for t in range(nt)], axis=0)
    bu = jnp.dot(sx.astype(BF16), bbd_ref[...], preferred_element_type=F32)
    lr = jnp.concatenate([lbr_ref[j] for j in range(ns)], axis=1)
    li = jnp.concatenate([lbi_ref[j] for j in range(ns)], axis=1)
    xr = s0r_ref[...]
    xi = s0i_ref[...]
    xs = []
    for t in range(nt):
        but = bu[t * nbat:(t + 1) * nbat]
        bur = jnp.concatenate([but[:, 2 * j * LANES:(2 * j + 1) * LANES] for j in range(ns)], axis=1)
        bui = jnp.concatenate([but[:, (2 * j + 1) * LANES:(2 * j + 2) * LANES] for j in range(ns)], axis=1)
        xr, xi = lr * xr - li * xi + bur, lr * xi + li * xr + bui
        xs.append(jnp.concatenate([xr, xi], axis=1).astype(BF16))
    sre_ref[...] = xr
    sim_ref[...] = xi
    s_out = _s5_out(jnp.concatenate(xs, axis=0), sx, cbd_ref, d_ref, wglu_ref, bglu_ref)
    for t in range(nt):
        o_ref[t, :, D_LRU:U_COLS] = s_out[t * nbat:(t + 1) * nbat]


def _rec_sample(u_t, h0, buf0_t, s0r, s0i, pr):
    nt, nbat, _ = u_t.shape
    params = [pr["cw"], pr["cb"], pr["wg"], pr["bg"], pr["c8"], pr["lbr"], pr["lbi"],
              pr["bbd"], pr["cbd"], pr["d"], pr["wglu"], pr["bglu"]]
    return pl.pallas_call(
        functools.partial(_rec_sample_kernel, nt=nt),
        out_shape=[jax.ShapeDtypeStruct((nt, nbat, U_COLS), F32),
                   jax.ShapeDtypeStruct((nbat, D_LRU), F32),
                   jax.ShapeDtypeStruct((CONV_W - 1, nbat, D_LRU), F32),
                   jax.ShapeDtypeStruct((nbat, S5_P), F32),
                   jax.ShapeDtypeStruct((nbat, S5_P), F32)],
        compiler_params=pltpu.CompilerParams(vmem_limit_bytes=VMEM_LIMIT),
        name="rec_sample",
    )(u_t, h0, buf0_t, s0r, s0i, *params)


def _rec_params(lru_conv_w, lru_conv_b, lru_w_r, lru_b_r, lru_w_i, lru_b_i, lru_lam,
                s5_a_re, s5_a_im, s5_log_dt, s5_b_re, s5_b_im, s5_c_re, s5_c_im, s5_d, s5_w_glu, s5_b_glu):
    ns = S5_P // LANES
    bd = lambda w: jax.scipy.linalg.block_diag(*[w[n] for n in range(w.shape[0])])
    wg = jnp.concatenate([bd(lru_w_r), bd(lru_w_i)], axis=1).astype(BF16)
    bg = jnp.concatenate([lru_b_r, lru_b_i]).reshape(1, 2 * D_LRU)
    c8 = (LRU_C * jax.nn.log_sigmoid(lru_lam.astype(F32))).reshape(1, D_LRU)
    dt = jnp.exp(s5_log_dt.astype(F32))[:, None]
    ar = s5_a_re.astype(F32)
    ai = s5_a_im.astype(F32)
    mag = jnp.exp(ar * dt)
    lb_re = mag * jnp.cos(ai * dt)
    lb_im = mag * jnp.sin(ai * dt)
    den = ar * ar + ai * ai
    n_re = lb_re - 1.0
    co_re = (n_re * ar + lb_im * ai) / den
    co_im = (lb_im * ar - n_re * ai) / den
    bb_re = co_re[..., None] * s5_b_re - co_im[..., None] * s5_b_im
    bb_im = co_re[..., None] * s5_b_im + co_im[..., None] * s5_b_re
    bre = bd(jnp.swapaxes(bb_re, 1, 2))
    bim = bd(jnp.swapaxes(bb_im, 1, 2))
    bbd = jnp.stack([bre.reshape(D_S5, ns, LANES), bim.reshape(D_S5, ns, LANES)], axis=2)
    bbd = bbd.reshape(D_S5, 2 * S5_P).astype(BF16)
    cre = bd(jnp.swapaxes(s5_c_re, 1, 2))
    cim = bd(jnp.swapaxes(s5_c_im, 1, 2))
    cbd = jnp.concatenate([cre, -cim], axis=0).astype(BF16)
    return dict(cw=lru_conv_w, cb=lru_conv_b.reshape(1, D_LRU), wg=wg, bg=bg, c8=c8,
                lbr=lb_re.reshape(ns, 1, LANES), lbi=lb_im.reshape(ns, 1, LANES),
                bbd=bbd, cbd=cbd, d=s5_d.reshape(1, D_S5),
                wglu=s5_w_glu.astype(BF16), bglu=s5_b_glu.reshape(1, D_S5))


def _moba_prompt_kernel(q_ref, k_ref, v_ref, o_ref, km_s, bias_s, *, s, nblk):
    qi = pl.program_id(2)
    tq = MOBA_BLOCK
    row = lax.broadcasted_iota(jnp.int32, (tq, tq), 0)
    col = lax.broadcasted_iota(jnp.int32, (tq, tq), 1)
    blk_iota = lax.broadcasted_iota(jnp.int32, (nblk, tq), 0)
    lane_blk = lax.broadcasted_iota(jnp.int32, (tq, LANES), 1)
    nt_dims = (((1,), (1,)), ((), ()))
    outs = []
    for hh in range(2):
        @pl.when(qi == 0)
        def _():
            r = lax.broadcasted_iota(jnp.int32, (nblk, s), 1) // MOBA_BLOCK
            n = lax.broadcasted_iota(jnp.int32, (nblk, s), 0)
            pm = jnp.where(r == n, 1.0 / MOBA_BLOCK, 0.0).astype(F32)
            km_s[hh] = jnp.dot(pm, k_ref[0, hh], precision=HIGHEST, preferred_element_type=F32)

        q = q_ref[0, hh]
        sb = lax.dot_general(km_s[hh], q, nt_dims, precision=HIGHEST, preferred_element_type=F32)
        past = blk_iota < qi
        sb = jnp.where(past, sb, -jnp.inf)
        cnt = jnp.zeros((nblk, tq), jnp.int32)
        for i in range(nblk):
            ri = sb[i:i + 1, :]
            beats = (ri > sb) | ((ri == sb) & (i < blk_iota))
            cnt = cnt + beats.astype(jnp.int32)
        sel = (cnt < MOBA_TOPK) & past
        bias = jnp.where(sel, 0.0, NEG).astype(F32)
        bias = jnp.concatenate([bias, jnp.zeros((LANES - nblk, tq), F32)], axis=0)
        bias_s[...] = jnp.transpose(bias)

        qs = (q * (HEAD_DIM ** -0.5)).astype(BF16)
        start = pl.multiple_of(qi * tq, tq)
        kj = k_ref[0, hh, pl.ds(start, tq), :].astype(BF16)
        vj = v_ref[0, hh, pl.ds(start, tq), :].astype(BF16)
        sc = lax.dot_general(qs, kj, nt_dims, preferred_element_type=F32)
        sc = jnp.where(col <= row, sc, NEG)
        m = jnp.max(sc, axis=1, keepdims=True)
        p = jnp.exp(sc - m)
        l = jnp.sum(p, axis=1, keepdims=True)
        acc = jnp.dot(p.astype(BF16), vj, preferred_element_type=F32)

        def body(j, carry):
            m, l, acc = carry
            st = pl.multiple_of(j * tq, tq)
            kj = k_ref[0, hh, pl.ds(st, tq), :].astype(BF16)
            vj = v_ref[0, hh, pl.ds(st, tq), :].astype(BF16)
            bias = jnp.sum(jnp.where(lane_blk == j, bias_s[...], 0.0), axis=1, keepdims=True)
            sc = lax.dot_general(qs, kj, nt_dims, preferred_element_type=F32) + bias
            m_new = jnp.maximum(m, jnp.max(sc, axis=1, keepdims=True))
            alpha = jnp.exp(m - m_new)
            p = jnp.exp(sc - m_new)
            l = alpha * l + jnp.sum(p, axis=1, keepdims=True)
            acc = alpha * acc + jnp.dot(p.astype(BF16), vj, preferred_element_type=F32)
            return m_new, l, acc

        m, l, acc = lax.fori_loop(0, qi, body, (m, l, acc))
        outs.append(acc / l)
    o_ref[0] = jnp.concatenate(outs, axis=1)


def _moba_prompt(q, k, v):
    nb, nh, s, hd = q.shape
    nblk = s // MOBA_BLOCK
    tq = MOBA_BLOCK
    return pl.pallas_call(
        functools.partial(_moba_prompt_kernel, s=s, nblk=nblk),
        grid=(nb, nh // 2, nblk),
        in_specs=[pl.BlockSpec((1, 2, tq, hd), lambda b, h, i: (b, h, i, 0)),
                  pl.BlockSpec((1, 2, s, hd), lambda b, h, i: (b, h, 0, 0)),
                  pl.BlockSpec((1, 2, s, hd), lambda b, h, i: (b, h, 0, 0))],
        out_specs=pl.BlockSpec((1, tq, 2 * hd), lambda b, h, i: (b, i, h)),
        out_shape=jax.ShapeDtypeStruct((nb, s, nh * hd), F32),
        scratch_shapes=[pltpu.VMEM((2, nblk, hd), F32), pltpu.VMEM((tq, LANES), F32)],
        compiler_params=_cparams(("arbitrary", "arbitrary", "arbitrary")),
        name="moba_prompt",
    )(q, k, v)


PAGES_PER_BLOCK = MOBA_BLOCK // PAGE_SIZE
N_PAGES = PAST_LEN // PAGE_SIZE
N_FULL = PAST_LEN // MOBA_BLOCK
PG = 16


def _page_copy(cache_ref, pt_ref, buf, sem, layer, step, slot, j):
    pid = pt_ref[step * PG + j]
    return pltpu.make_async_copy(cache_ref.at[layer, pid], buf.at[slot, j], sem.at[slot, j])


def _ksum_kernel(pt_ref, cache_ref, o_ref, buf, sem, *, layer, nsteps):
    n = pl.program_id(0)
    slot = n % 2

    def start(step, sl):
        for j in range(PG):
            _page_copy(cache_ref, pt_ref, buf, sem, layer, step, sl, j).start()

    @pl.when(n == 0)
    def _():
        start(0, 0)

    @pl.when(n + 1 < nsteps)
    def _():
        start(n + 1, 1 - slot)

    for j in range(PG):
        _page_copy(cache_ref, pt_ref, buf, sem, layer, n, slot, j).wait()
    for blk in range(PG // PAGES_PER_BLOCK):
        for hh in range(N_HEADS):
            tot = jnp.sum(buf[slot, PAGES_PER_BLOCK * blk, hh], axis=0, keepdims=True)
            for pp in range(1, PAGES_PER_BLOCK):
                tot = tot + jnp.sum(buf[slot, PAGES_PER_BLOCK * blk + pp, hh], axis=0, keepdims=True)
            o_ref[0, hh, blk:blk + 1, :] = tot


def _block_ksum(cache_k, pt_flat, layer, nbat):
    steps_per_b = N_PAGES // PG
    nsteps = nbat * steps_per_b
    bpg = PG // PAGES_PER_BLOCK
    return pl.pallas_call(
        functools.partial(_ksum_kernel, layer=layer, nsteps=nsteps),
        grid_spec=pltpu.PrefetchScalarGridSpec(
            num_scalar_prefetch=1,
            grid=(nsteps,),
            in_specs=[pl.BlockSpec(memory_space=pl.ANY)],
            out_specs=pl.BlockSpec((1, N_HEADS, bpg, HEAD_DIM),
                                   lambda n, pt: (n // steps_per_b, 0, n % steps_per_b, 0)),
            scratch_shapes=[pltpu.VMEM((2, PG, N_HEADS, PAGE_SIZE, HEAD_DIM), F32),
                            pltpu.SemaphoreType.DMA((2, PG))]),
        out_shape=jax.ShapeDtypeStruct((nbat, N_HEADS, N_FULL, HEAD_DIM), F32),
        compiler_params=_cparams(("arbitrary",)),
        name="block_ksum",
    )(pt_flat, cache_k)


def _topk_kernel(q_ref, ks_ref, o_ref):
    nt = q_ref.shape[2]
    lane = lax.broadcasted_iota(jnp.int32, (nt, N_FULL), 1)
    out_lane = lax.broadcasted_iota(jnp.int32, (nt, LANES), 1)
    for hh in range(N_HEADS):
        km = ks_ref[0, hh] * (1.0 / MOBA_BLOCK)
        sb = lax.dot_general(q_ref[0, hh], km, (((1,), (1,)), ((), ())), precision=HIGHEST,
                             preferred_element_type=F32)
        out = jnp.zeros((nt, LANES), jnp.int32)
        for r in range(MOBA_TOPK):
            mx = jnp.max(sb, axis=1, keepdims=True)
            idx = jnp.min(jnp.where(sb == mx, lane, N_FULL), axis=1, keepdims=True)
            out = jnp.where(out_lane == r, idx, out)
            sb = jnp.where(lane == idx, -jnp.inf, sb)
        o_ref[0, hh] = out


def _sample_topk(q, ksum):
    nbat, nh, nt, hd = q.shape
    return pl.pallas_call(
        _topk_kernel,
        grid=(nbat,),
        in_specs=[pl.BlockSpec((1, nh, nt, hd), lambda b: (b, 0, 0, 0)),
                  pl.BlockSpec((1, nh, N_FULL, hd), lambda b: (b, 0, 0, 0))],
        out_specs=pl.BlockSpec((1, nh, nt, LANES), lambda b: (b, 0, 0, 0)),
        out_shape=jax.ShapeDtypeStruct((nbat, nh, nt, LANES), jnp.int32),
        compiler_params=_cparams(("arbitrary",)),
        name="sample_topk",
    )(q, ksum)


def _sel_copy(cache_ref, idx_ref, pt_ref, buf, sem, layer, step, slot, t, r, pp, nt):
    b = step // N_HEADS
    hh = step % N_HEADS
    blk = idx_ref[(step * nt + t) * MOBA_TOPK + r]
    pid = pt_ref[b * N_PAGES + blk * PAGES_PER_BLOCK + pp]
    j = (t * MOBA_TOPK + r) * PAGES_PER_BLOCK + pp
    return pltpu.make_async_copy(cache_ref.at[layer, pid, hh], buf.at[slot, j], sem.at[slot, j])


def _moba_sample_kernel(idx_ref, pt_ref, q_ref, kn_ref, vn_ref, ck_ref, cv_ref, o_ref,
                        kbuf, vbuf, ksem, vsem, *, layer, nsteps, nt):
    n = pl.program_id(0)
    slot = n % 2
    nsel = MOBA_TOPK * PAGES_PER_BLOCK
    own = nt * nsel

    def copies(step, sl):
        out = []
        for t in range(nt):
            for r in range(MOBA_TOPK):
                for pp in range(PAGES_PER_BLOCK):
                    out.append(_sel_copy(ck_ref, idx_ref, pt_ref, kbuf, ksem, layer, step, sl, t, r, pp, nt))
                    out.append(_sel_copy(cv_ref, idx_ref, pt_ref, vbuf, vsem, layer, step, sl, t, r, pp, nt))
        return out

    @pl.when(n == 0)
    def _():
        for sl in range(2):
            kbuf[sl, own] = jnp.zeros((PAGE_SIZE, HEAD_DIM), F32)
            vbuf[sl, own] = jnp.zeros((PAGE_SIZE, HEAD_DIM), F32)
        for cp in copies(0, 0):
            cp.start()

    @pl.when(n + 1 < nsteps)
    def _():
        for cp in copies(n + 1, 1 - slot):
            cp.start()

    kbuf[slot, own, 0:nt, :] = kn_ref[0, 0]
    vbuf[slot, own, 0:nt, :] = vn_ref[0, 0]
    for cp in copies(n, slot):
        cp.wait()

    qs = (q_ref[0, 0] * (HEAD_DIM ** -0.5)).astype(BF16)
    nq = qs.shape[0]
    nt_dims = (((1,), (1,)), ((), ()))
    nkeys = nsel * PAGE_SIZE
    qrow = lax.broadcasted_iota(jnp.int32, (nq, nkeys), 0)
    scores = []
    for t in range(nt):
        kt = kbuf[slot, t * nsel:(t + 1) * nsel].reshape(nkeys, HEAD_DIM).astype(BF16)
        st = lax.dot_general(qs, kt, nt_dims, preferred_element_type=F32)
        scores.append(jnp.where(qrow == t, st, -jnp.inf))
    ko = kbuf[slot, own].astype(BF16)
    so = lax.dot_general(qs, ko, nt_dims, preferred_element_type=F32)
    orow = lax.broadcasted_iota(jnp.int32, (nq, PAGE_SIZE), 0)
    ocol = lax.broadcasted_iota(jnp.int32, (nq, PAGE_SIZE), 1)
    so = jnp.where(ocol <= orow, so, -jnp.inf)
    m = jnp.max(so, axis=1, keepdims=True)
    for st in scores:
        m = jnp.maximum(m, jnp.max(st, axis=1, keepdims=True))
    po = jnp.exp(so - m)
    den = jnp.sum(po, axis=1, keepdims=True)
    acc = jnp.dot(po.astype(BF16), vbuf[slot, own].astype(BF16), preferred_element_type=F32)
    for t in range(nt):
        pt_ = jnp.exp(scores[t] - m)
        den = den + jnp.sum(pt_, axis=1, keepdims=True)
        vt = vbuf[slot, t * nsel:(t + 1) * nsel].reshape(nkeys, HEAD_DIM).astype(BF16)
        acc = acc + jnp.dot(pt_.astype(BF16), vt, preferred_element_type=F32)
    o_ref[0, 0] = acc / den


def _moba_sample(q_pad, k_new, v_new, cache_k, cache_v, idx_flat, pt_flat, layer):
    nbat, nh, nq, hd = q_pad.shape
    nt = k_new.shape[2]
    nsteps = nbat * nh
    nslab = nt * MOBA_TOPK * PAGES_PER_BLOCK + 1
    tok = pl.BlockSpec((1, 1, nt, hd), lambda n, idx, pt: (n // nh, n % nh, 0, 0))
    qtok = pl.BlockSpec((1, 1, nq, hd), lambda n, idx, pt: (n // nh, n % nh, 0, 0))
    return pl.pallas_call(
        functools.partial(_moba_sample_kernel, layer=layer, nsteps=nsteps, nt=nt),
        grid_spec=pltpu.PrefetchScalarGridSpec(
            num_scalar_prefetch=2,
            grid=(nsteps,),
            in_specs=[qtok, tok, tok, pl.BlockSpec(memory_space=pl.ANY), pl.BlockSpec(memory_space=pl.ANY)],
            out_specs=qtok,
            scratch_shapes=[pltpu.VMEM((2, nslab, PAGE_SIZE, hd), F32),
                            pltpu.VMEM((2, nslab, PAGE_SIZE, hd), F32),
                            pltpu.SemaphoreType.DMA((2, nslab - 1)),
                            pltpu.SemaphoreType.DMA((2, nslab - 1))]),
        out_shape=jax.ShapeDtypeStruct((nbat, nh, nq, hd), F32),
        compiler_params=_cparams(("arbitrary",)),
        name="moba_sample",
    )(idx_flat, pt_flat, q_pad, k_new, v_new, cache_k, cache_v)


def _rope_tables(pos):
    inv = jnp.power(ROPE_THETA, -jnp.arange(HALF, dtype=F32) / HALF)
    ang = pos.astype(F32)[:, None] * inv[None, :]
    cos, sin = jnp.cos(ang), jnp.sin(ang)
    reps = LANES // HEAD_DIM
    return (jnp.tile(jnp.concatenate([cos, cos], axis=1), (1, reps)),
            jnp.tile(jnp.concatenate([-sin, sin], axis=1), (1, reps)))


def _permute_w_in(w):
    a_x, a_g, q, k, v, b_g, s_x, s_g = jnp.split(
        w, (D_LRU, 2 * D_LRU, 2 * D_LRU + D_ATT, 2 * D_LRU + 2 * D_ATT, 2 * D_LRU + 3 * D_ATT,
            2 * D_LRU + 4 * D_ATT, 2 * D_LRU + 4 * D_ATT + D_S5), axis=1)
    return jnp.concatenate([a_x, s_x, a_g, b_g, s_g, q, k, v], axis=1).astype(BF16)


def kernel(x_prompt, x_sample, cache_k, cache_v, state_lru_h, state_lru_conv, state_s5_re, state_s5_im,
           page_table, c_prompt, c_sample, norm_g, w_ada, b_ada, w_in, w_out, lru_conv_w, lru_conv_b,
           lru_w_r, lru_b_r, lru_w_i, lru_b_i, lru_lam, s5_a_re, s5_a_im, s5_log_dt, s5_b_re, s5_b_im,
           s5_c_re, s5_c_im, s5_d, s5_w_glu, s5_b_glu, final_g):
    nbp, seq, _ = x_prompt.shape
    nbs, nts, _ = x_sample.shape
    ntok_s = nbs * nts

    cos_p, sin_p = _rope_tables(jnp.arange(seq, dtype=jnp.int32))
    pos_s = PAST_LEN + (jnp.arange(ntok_s, dtype=jnp.int32) % nts)
    cos_s, sin_s = _rope_tables(pos_s)

    n_c = nbp + nbs
    c_all = jnp.concatenate([c_prompt, c_sample, jnp.zeros((-n_c % 8, D_MODEL), F32)], axis=0)
    mod = _modulation(c_all, w_ada, b_ada)

    pt_flat = page_table.reshape(-1)
    xp = x_prompt
    xs = x_sample.reshape(1, ntok_s, D_MODEL)
    outs_p = [[] for _ in range(6)]
    outs_s = [[] for _ in range(6)]
    for l in range(DEPTH):
        last = l == DEPTH - 1
        w_perm = _permute_w_in(w_in[l])
        w_o = w_out[l].astype(BF16)
        pr = _rec_params(lru_conv_w[l], lru_conv_b[l], lru_w_r[l], lru_b_r[l], lru_w_i[l], lru_b_i[l],
                         lru_lam[l], s5_a_re[l], s5_a_im[l], s5_log_dt[l], s5_b_re[l], s5_b_im[l],
                         s5_c_re[l], s5_c_im[l], s5_d[l], s5_w_glu[l], s5_b_glu[l])
        shift, scale, gate = jnp.split(mod[l], 3, axis=-1)

        mp = lambda a: a[:nbp].reshape(nbp, 1, D_MODEL)
        u, gates, q, k, v = _inproj(xp, mp(scale), mp(shift), norm_g[l], w_perm, cos_p, sin_p, tm=512)
        as_out, h_last, cbuf, s_re, s_im = _rec_prompt(u, pr, tc=256)
        att = _moba_prompt(q, k, v)
        xp = _outproj(xp, as_out, att, gates, mp(gate), w_o, final_g, tm=512, final=last)
        for lst, val in zip(outs_p, (k, v, h_last, cbuf, s_re.reshape(nbp, S5_GROUPS, S5_STATE),
                                     s_im.reshape(nbp, S5_GROUPS, S5_STATE))):
            lst.append(val)

        ms = lambda a: jnp.repeat(a[nbp:n_c], nts, axis=0).reshape(1, ntok_s, D_MODEL)
        u, gates, q, k, v = _inproj(xs, ms(scale), ms(shift), norm_g[l], w_perm, cos_s, sin_s, tm=ntok_s)
        to_heads = lambda a: a.reshape(N_HEADS, nbs, nts, HEAD_DIM).transpose(1, 0, 2, 3)
        q, k, v = to_heads(q), to_heads(k), to_heads(v)
        u_t = u.reshape(nbs, nts, U_COLS).transpose(1, 0, 2)
        as_t, h_last, cbuf_t, s_re, s_im = _rec_sample(
            u_t, state_lru_h[l], state_lru_conv[l].transpose(1, 0, 2),
            state_s5_re[l].reshape(nbs, S5_P), state_s5_im[l].reshape(nbs, S5_P), pr)
        as_out = as_t.transpose(1, 0, 2).reshape(1, ntok_s, U_COLS)
        ksum = _block_ksum(cache_k, pt_flat, l, nbs)
        q_pad = jnp.pad(q, ((0, 0), (0, 0), (0, -nts % 8), (0, 0)))
        idx = _sample_topk(q_pad, ksum)[:, :, :nts, :MOBA_TOPK].reshape(-1)
        att = _moba_sample(q_pad, k, v, cache_k, cache_v, idx, pt_flat, l)[:, :, :nts]
        att = att.transpose(0, 2, 1, 3).reshape(1, ntok_s, D_ATT)
        xs = _outproj(xs, as_out, att, gates, ms(gate), w_o, final_g, tm=ntok_s, final=last)
        for lst, val in zip(outs_s, (k, v, h_last, cbuf_t.transpose(1, 0, 2),
                                     s_re.reshape(nbs, S5_GROUPS, S5_STATE),
                                     s_im.reshape(nbs, S5_GROUPS, S5_STATE))):
            lst.append(val)

    return (xp, xs.reshape(nbs, nts, D_MODEL),
            *[jnp.stack(o) for o in outs_p], *[jnp.stack(o) for o in outs_s])
```

```python
import functools

import jax
import jax.numpy as jnp
from jax import lax
from jax.experimental import pallas as pl
from jax.experimental.pallas import tpu as pltpu

F32 = jnp.float32
BF16 = jnp.bfloat16
HIGHEST = lax.Precision.HIGHEST

D_MODEL = 1024
DEPTH = 2
PAST_LEN = 16384
PAGE_SIZE = 128
D_LRU = 256
LRU_BLOCKS = 4
LRU_BW = D_LRU // LRU_BLOCKS
CONV_W = 4
LRU_C = 8.0
N_HEADS = 8
HEAD_DIM = 64
HALF = HEAD_DIM // 2
D_ATT = N_HEADS * HEAD_DIM
MOBA_BLOCK = 256
MOBA_TOPK = 3
ROPE_THETA = 10000.0
D_S5 = 256
S5_GROUP = 16
S5_GROUPS = D_S5 // S5_GROUP
S5_STATE = 64
S5_P = S5_GROUPS * S5_STATE
D_MIX = D_LRU + D_ATT + D_S5
EPS = 1e-6

LANES = 128
BF16_ROWS = 16
NEG = -1e30
LOG2E = 1.4426950408889634
VB_ROWS = HEAD_DIM + BF16_ROWS

U_COLS = D_LRU + D_S5
G_COLS = D_MIX
OFF_U = 0
OFF_G = OFF_U + U_COLS
OFF_KR = OFF_G + G_COLS

VMEM_LIMIT = 56 * 1024 * 1024

NT_DIMS = (((1,), (1,)), ((), ()))


def _cparams(sem):
    return pltpu.CompilerParams(dimension_semantics=sem, vmem_limit_bytes=VMEM_LIMIT)


def _silu(x):
    return x * jax.nn.sigmoid(x)


def _mod_kernel(c_ref, w_ref, b_ref, o_ref):
    o_ref[0] = jnp.dot(_silu(c_ref[...]), w_ref[0], precision=HIGHEST,
                       preferred_element_type=F32) + b_ref[0]


def _modulation(c_all, w_ada, b_ada):
    n = c_all.shape[0]
    tn = D_MODEL
    return pl.pallas_call(
        _mod_kernel,
        grid=(DEPTH, 3 * D_MODEL // tn),
        in_specs=[pl.BlockSpec((n, D_MODEL), lambda l, j: (0, 0)),
                  pl.BlockSpec((1, D_MODEL, tn), lambda l, j: (l, 0, j)),
                  pl.BlockSpec((1, 1, tn), lambda l, j: (l, 0, j))],
        out_specs=pl.BlockSpec((1, n, tn), lambda l, j: (l, 0, j)),
        out_shape=jax.ShapeDtypeStruct((DEPTH, n, 3 * D_MODEL), F32),
        compiler_params=_cparams(("arbitrary", "arbitrary")),
        name="modulation",
    )(c_all, w_ada, b_ada.reshape(DEPTH, 1, 3 * D_MODEL))


def _inproj_kernel(x_ref, sc_ref, sh_ref, g_ref, w_ref, wt_ref, cos_ref, sin_ref, cost_ref, sint_ref,
                   u_ref, gate_ref, krm_ref, qt_ref, kt_ref, vt_ref):
    x = x_ref[0]
    h = x * lax.rsqrt(jnp.mean(x * x, axis=-1, keepdims=True) + EPS) * g_ref[...]
    h = h * (1.0 + sc_ref[0]) + sh_ref[0]
    hb = h.astype(BF16)

    def proj(lo, n):
        return jnp.dot(hb, w_ref[:, lo:lo + n], preferred_element_type=F32)

    u_ref[0] = proj(OFF_U, U_COLS)
    gate_ref[0] = _silu(proj(OFF_G, G_COLS))

    cos = cos_ref[...]
    sin = sin_ref[...]
    lane = lax.broadcasted_iota(jnp.int32, cos.shape, 1)
    first_half = (lane & (HEAD_DIM - 1)) < HALF
    p = proj(OFF_KR, D_ATT)
    for j in range(D_ATT // LANES):
        xs = p[:, j * LANES:(j + 1) * LANES]
        partner = jnp.where(first_half, pltpu.roll(xs, LANES - HALF, 1), pltpu.roll(xs, HALF, 1))
        krm_ref[0, :, j * LANES:(j + 1) * LANES] = (xs * cos + partner * sin).astype(BF16)

    cost = cost_ref[...]
    sint = sint_ref[...]
    for n, (ref, use_rope) in enumerate(((qt_ref, True), (kt_ref, True), (vt_ref, False))):
        pt = lax.dot_general(wt_ref[n * D_ATT:(n + 1) * D_ATT, :], hb, NT_DIMS,
                             preferred_element_type=F32)
        if not use_rope:
            ref[0] = pt
            continue
        for hh in range(N_HEADS):
            r0 = hh * HEAD_DIM
            x1 = pt[r0:r0 + HALF]
            x2 = pt[r0 + HALF:r0 + HEAD_DIM]
            ref[0, r0:r0 + HALF, :] = x1 * cost - x2 * sint
            ref[0, r0 + HALF:r0 + HEAD_DIM, :] = x2 * cost + x1 * sint


def _inproj(x, scale, shift, g, w_row, w_t, rope, tm):
    nb, s, _ = x.shape
    cos_r, sin_r, cos_t, sin_t = rope
    rm = scale.shape[1]
    mod_block = (1, tm, D_MODEL) if rm == s else (1, 1, D_MODEL)
    mod_map = (lambda b, i: (b, i, 0)) if rm == s else (lambda b, i: (b, 0, 0))
    fm_spec = pl.BlockSpec((1, D_ATT, tm), lambda b, i: (b, 0, i))
    fm_shape = jax.ShapeDtypeStruct((nb, D_ATT, s), F32)
    return pl.pallas_call(
        _inproj_kernel,
        grid=(nb, s // tm),
        in_specs=[pl.BlockSpec((1, tm, D_MODEL), lambda b, i: (b, i, 0)),
                  pl.BlockSpec(mod_block, mod_map),
                  pl.BlockSpec(mod_block, mod_map),
                  pl.BlockSpec((1, D_MODEL), lambda b, i: (0, 0)),
                  pl.BlockSpec(w_row.shape, lambda b, i: (0, 0)),
                  pl.BlockSpec(w_t.shape, lambda b, i: (0, 0)),
                  pl.BlockSpec((tm, LANES), lambda b, i: (i, 0)),
                  pl.BlockSpec((tm, LANES), lambda b, i: (i, 0)),
                  pl.BlockSpec((HALF, tm), lambda b, i: (0, i)),
                  pl.BlockSpec((HALF, tm), lambda b, i: (0, i))],
        out_specs=[pl.BlockSpec((1, tm, U_COLS), lambda b, i: (b, i, 0)),
                   pl.BlockSpec((1, tm, G_COLS), lambda b, i: (b, i, 0)),
                   pl.BlockSpec((1, tm, D_ATT), lambda b, i: (b, i, 0)),
                   fm_spec, fm_spec, fm_spec],
        out_shape=[jax.ShapeDtypeStruct((nb, s, U_COLS), F32),
                   jax.ShapeDtypeStruct((nb, s, G_COLS), F32),
                   jax.ShapeDtypeStruct((nb, s, D_ATT), BF16),
                   fm_shape, fm_shape, fm_shape],
        compiler_params=_cparams(("arbitrary", "arbitrary")),
        name="inproj",
    )(x, scale, shift, g.reshape(1, D_MODEL), w_row, w_t, cos_r, sin_r, cos_t, sin_t)


def _outproj_kernel(x_ref, as_ref, att_ref, gate_ref, gm_ref, w_ref, fg_ref, o_ref, *, final):
    g = gate_ref[0]
    a_s = as_ref[0]
    mixed = jnp.concatenate([a_s[:, :D_LRU] * g[:, :D_LRU],
                             att_ref[0] * g[:, D_LRU:D_LRU + D_ATT],
                             a_s[:, D_LRU:] * g[:, D_LRU + D_ATT:]], axis=1).astype(BF16)
    y = jnp.dot(mixed, w_ref[...], preferred_element_type=F32)
    xn = x_ref[0] + gm_ref[0] * y
    if final:
        xn = xn * lax.rsqrt(jnp.mean(xn * xn, axis=-1, keepdims=True) + EPS) * fg_ref[...]
    o_ref[0] = xn


def _outproj(x, as_out, att, gates, gate_mod, w_out, final_g, tm, final):
    nb, s, _ = x.shape
    rm = gate_mod.shape[1]
    mod_block = (1, tm, D_MODEL) if rm == s else (1, 1, D_MODEL)
    mod_map = (lambda b, i: (b, i, 0)) if rm == s else (lambda b, i: (b, 0, 0))
    row = lambda n: pl.BlockSpec((1, tm, n), lambda b, i: (b, i, 0))
    return pl.pallas_call(
        functools.partial(_outproj_kernel, final=final),
        grid=(nb, s // tm),
        in_specs=[row(D_MODEL), row(U_COLS), row(D_ATT), row(G_COLS),
                  pl.BlockSpec(mod_block, mod_map),
                  pl.BlockSpec((D_MIX, D_MODEL), lambda b, i: (0, 0)),
                  pl.BlockSpec((1, D_MODEL), lambda b, i: (0, 0))],
        out_specs=row(D_MODEL),
        out_shape=jax.ShapeDtypeStruct((nb, s, D_MODEL), F32),
        compiler_params=_cparams(("arbitrary", "arbitrary")),
        name="outproj",
    )(x, as_out, att, gates, gate_mod, w_out, final_g.reshape(1, D_MODEL))


def _lru_gates(xc, wg_ref, bg_ref, c8_ref):
    gates = jnp.dot(xc.astype(BF16), wg_ref[...], preferred_element_type=F32) + bg_ref[...]
    r = jax.nn.sigmoid(gates[:, :D_LRU])
    i = jax.nn.sigmoid(gates[:, D_LRU:])
    log_a = r * c8_ref[...]
    a = jnp.exp(log_a)
    mult = jnp.sqrt(1.0 - jnp.exp(2.0 * log_a))
    return a, mult * i * xc


def _s5_out(x_cat_bf16, sx, cbd_ref, d_ref, wglu_ref, bglu_ref):
    y = jnp.dot(x_cat_bf16, cbd_ref[...], preferred_element_type=F32) + d_ref[...] * sx
    z = jax.nn.gelu(y)
    return z * jax.nn.sigmoid(jnp.dot(z.astype(BF16), wglu_ref[...], preferred_element_type=F32)
                              + bglu_ref[...])


def _rec_prompt_kernel(u_ref, cw_ref, cb_ref, wg_ref, bg_ref, c8_ref, lbr_ref, lbi_ref,
                       bbd_ref, cbd_ref, d_ref, wglu_ref, bglu_ref,
                       o_ref, hl_ref, cbuf_ref, sre_ref, sim_ref,
                       ext, a_s, b_s, h_s, bu_s, x_s, h_st, x_st, *, nb, tc, pitch):
    c = pl.program_id(0)
    nl = D_LRU // LANES
    ns = S5_P // LANES

    @pl.when(c == 0)
    def _():
        ext[:, 0:8, :] = jnp.zeros((nb, 8, D_LRU), F32)
        h_st[...] = jnp.zeros(h_st.shape, F32)
        x_st[...] = jnp.zeros(x_st.shape, F32)

    for b in range(nb):
        r0 = b * pitch
        ext[b, 8:8 + tc, :] = u_ref[b, :, 0:D_LRU]
        xc = cb_ref[...]
        for j in range(CONV_W):
            xc = xc + ext[b, 8 - (CONV_W - 1) + j:8 - (CONV_W - 1) + j + tc, :] * cw_ref[j:j + 1, :]
        ext[b, 8 - (CONV_W - 1):8, :] = ext[b, 8 + tc - (CONV_W - 1):8 + tc, :]
        a, bb = _lru_gates(xc, wg_ref, bg_ref, c8_ref)
        for j in range(nl):
            a_s[j, r0:r0 + tc, :] = a[:, j * LANES:(j + 1) * LANES]
            b_s[j, r0:r0 + tc, :] = bb[:, j * LANES:(j + 1) * LANES]
        sxb = u_ref[b, :, D_LRU:U_COLS].astype(BF16)
        for j in range(ns):
            bu = jnp.dot(sxb, bbd_ref[:, 2 * j * LANES:(2 * j + 2) * LANES], preferred_element_type=F32)
            bu_s[j, r0:r0 + tc, :] = bu[:, :LANES]
            bu_s[ns + j, r0:r0 + tc, :] = bu[:, LANES:]

    def step(t, carry):
        h, xr, xi = carry
        rows = pl.ds(t, nb, stride=pitch)
        h_new, xr_new, xi_new = [], [], []
        for j in range(nl):
            hj = a_s[j, rows, :] * h[j] + b_s[j, rows, :]
            h_s[j, rows, :] = hj
            h_new.append(hj)
        for j in range(ns):
            lr = lbr_ref[j]
            li = lbi_ref[j]
            nr = lr * xr[j] - li * xi[j] + bu_s[j, rows, :]
            ni = lr * xi[j] + li * xr[j] + bu_s[ns + j, rows, :]
            x_s[j, rows, :] = nr
            x_s[ns + j, rows, :] = ni
            xr_new.append(nr)
            xi_new.append(ni)
        return tuple(h_new), tuple(xr_new), tuple(xi_new)

    init = (tuple(h_st[j] for j in range(nl)),
            tuple(x_st[j] for j in range(ns)),
            tuple(x_st[ns + j] for j in range(ns)))
    h, xr, xi = lax.fori_loop(0, tc, step, init)
    for j in range(nl):
        h_st[j] = h[j]
        hl_ref[:, j * LANES:(j + 1) * LANES] = h[j]
    for j in range(ns):
        x_st[j] = xr[j]
        x_st[ns + j] = xi[j]
        sre_ref[:, j * LANES:(j + 1) * LANES] = xr[j]
        sim_ref[:, j * LANES:(j + 1) * LANES] = xi[j]

    for b in range(nb):
        r0 = b * pitch
        for j in range(nl):
            o_ref[b, :, j * LANES:(j + 1) * LANES] = h_s[j, r0:r0 + tc, :]
        x_cat = jnp.concatenate([x_s[j, r0:r0 + tc, :].astype(BF16) for j in range(2 * ns)], axis=1)
        o_ref[b, :, D_LRU:U_COLS] = _s5_out(x_cat, u_ref[b, :, D_LRU:U_COLS], cbd_ref, d_ref,
                                            wglu_ref, bglu_ref)
        cbuf_ref[b] = ext[b, 8 - (CONV_W - 1):8, :]


def _rec_prompt(u, pr, tc):
    nb, s, _ = u.shape
    pitch = tc + 8
    nl = D_LRU // LANES
    ns = S5_P // LANES
    full = lambda a: pl.BlockSpec(a.shape, lambda c, _n=a.ndim: (0,) * _n)
    params = [pr["cw"], pr["cb"], pr["wg"], pr["bg"], pr["c8"], pr["lbr"], pr["lbi"],
              pr["bbd"], pr["cbd"], pr["d"], pr["wglu"], pr["bglu"]]
    st = lambda *shape: pl.BlockSpec(shape, lambda c, _n=len(shape): (0,) * _n)
    return pl.pallas_call(
        functools.partial(_rec_prompt_kernel, nb=nb, tc=tc, pitch=pitch),
        grid=(s // tc,),
        in_specs=[pl.BlockSpec((nb, tc, U_COLS), lambda c: (0, c, 0))] + [full(a) for a in params],
        out_specs=[pl.BlockSpec((nb, tc, U_COLS), lambda c: (0, c, 0)),
                   st(nb, D_LRU), st(nb, CONV_W - 1, D_LRU), st(nb, S5_P), st(nb, S5_P)],
        out_shape=[jax.ShapeDtypeStruct((nb, s, U_COLS), F32),
                   jax.ShapeDtypeStruct((nb, D_LRU), F32),
                   jax.ShapeDtypeStruct((nb, CONV_W - 1, D_LRU), F32),
                   jax.ShapeDtypeStruct((nb, S5_P), F32),
                   jax.ShapeDtypeStruct((nb, S5_P), F32)],
        scratch_shapes=[pltpu.VMEM((nb, tc + 8, D_LRU), F32),
                        pltpu.VMEM((nl, nb * pitch, LANES), F32),
                        pltpu.VMEM((nl, nb * pitch, LANES), F32),
                        pltpu.VMEM((nl, nb * pitch, LANES), F32),
                        pltpu.VMEM((2 * ns, nb * pitch, LANES), F32),
                        pltpu.VMEM((2 * ns, nb * pitch, LANES), F32),
                        pltpu.VMEM((nl, nb, LANES), F32),
                        pltpu.VMEM((2 * ns, nb, LANES), F32)],
        compiler_params=_cparams(("arbitrary",)),
        name="rec_prompt",
    )(u, *params)


def _rec_sample_kernel(u_ref, h0_ref, buf0_ref, s0r_ref, s0i_ref,
                       cw_ref, cb_ref, wg_ref, bg_ref, c8_ref, lbr_ref, lbi_ref,
                       bbd_ref, cbd_ref, d_ref, wglu_ref, bglu_ref,
                       o_ref, hl_ref, cbuf_ref, sre_ref, sim_ref, *, nt):
    ns = S5_P // LANES
    ext = [buf0_ref[j] for j in range(CONV_W - 1)] + [u_ref[t, :, 0:D_LRU] for t in range(nt)]
    xcs = []
    for t in range(nt):
        xc = cb_ref[...]
        for j in range(CONV_W):
            xc = xc + ext[t + j] * cw_ref[j:j + 1, :]
        xcs.append(xc)
    a, bb = _lru_gates(jnp.concatenate(xcs, axis=0), wg_ref, bg_ref, c8_ref)
    nbat = h0_ref.shape[0]
    h = h0_ref[...]
    for t in range(nt):
        h = a[t * nbat:(t + 1) * nbat] * h + bb[t * nbat:(t + 1) * nbat]
        o_ref[t, :, 0:D_LRU] = h
    hl_ref[...] = h
    for j in range(CONV_W - 1):
        cbuf_ref[j] = ext[nt + j]

    sx = jnp.concatenate([u_ref[t, :, D_LRU:U_COLS] for t in range(nt)], axis=0)
    bu = jnp.dot(sx.astype(BF16), bbd_ref[...], preferred_element_type=F32)
    lr = jnp.concatenate([lbr_ref[j] for j in range(ns)], axis=1)
    li = jnp.concatenate([lbi_ref[j] for j in range(ns)], axis=1)
    xr = s0r_ref[...]
    xi = s0i_ref[...]
    xs = []
    for t in range(nt):
        but = bu[t * nbat:(t + 1) * nbat]
        bur = jnp.concatenate([but[:, 2 * j * LANES:(2 * j + 1) * LANES] for j in range(ns)], axis=1)
        bui = jnp.concatenate([but[:, (2 * j + 1) * LANES:(2 * j + 2) * LANES] for j in range(ns)], axis=1)
        xr, xi = lr * xr - li * xi + bur, lr * xi + li * xr + bui
        xs.append(jnp.concatenate([xr, xi], axis=1).astype(BF16))
    sre_ref[...] = xr
    sim_ref[...] = xi
    s_out = _s5_out(jnp.concatenate(xs, axis=0), sx, cbd_ref, d_ref, wglu_ref, bglu_ref)
    for t in range(nt):
        o_ref[t, :, D_LRU:U_COLS] = s_out[t * nbat:(t + 1) * nbat]


def _rec_sample(u_t, h0, buf0_t, s0r, s0i, pr):
    nt, nbat, _ = u_t.shape
    params = [pr["cw"], pr["cb"], pr["wg"], pr["bg"], pr["c8"], pr["lbr"], pr["lbi"],
              pr["bbd"], pr["cbd"], pr["d"], pr["wglu"], pr["bglu"]]
    return pl.pallas_call(
        functools.partial(_rec_sample_kernel, nt=nt),
        out_shape=[jax.ShapeDtypeStruct((nt, nbat, U_COLS), F32),
                   jax.ShapeDtypeStruct((nbat, D_LRU), F32),
                   jax.ShapeDtypeStruct((CONV_W - 1, nbat, D_LRU), F32),
                   jax.ShapeDtypeStruct((nbat, S5_P), F32),
                   jax.ShapeDtypeStruct((nbat, S5_P), F32)],
        compiler_params=pltpu.CompilerParams(vmem_limit_bytes=VMEM_LIMIT),
        name="rec_sample",
    )(u_t, h0, buf0_t, s0r, s0i, *params)


def _rec_params(lru_conv_w, lru_conv_b, lru_w_r, lru_b_r, lru_w_i, lru_b_i, lru_lam,
                s5_a_re, s5_a_im, s5_log_dt, s5_b_re, s5_b_im, s5_c_re, s5_c_im, s5_d, s5_w_glu, s5_b_glu):
    ns = S5_P // LANES
    bd = lambda w: jax.scipy.linalg.block_diag(*[w[n] for n in range(w.shape[0])])
    wg = jnp.concatenate([bd(lru_w_r), bd(lru_w_i)], axis=1).astype(BF16)
    bg = jnp.concatenate([lru_b_r, lru_b_i]).reshape(1, 2 * D_LRU)
    c8 = (LRU_C * jax.nn.log_sigmoid(lru_lam.astype(F32))).reshape(1, D_LRU)
    dt = jnp.exp(s5_log_dt.astype(F32))[:, None]
    ar = s5_a_re.astype(F32)
    ai = s5_a_im.astype(F32)
    mag = jnp.exp(ar * dt)
    lb_re = mag * jnp.cos(ai * dt)
    lb_im = mag * jnp.sin(ai * dt)
    den = ar * ar + ai * ai
    n_re = lb_re - 1.0
    co_re = (n_re * ar + lb_im * ai) / den
    co_im = (lb_im * ar - n_re * ai) / den
    bb_re = co_re[..., None] * s5_b_re - co_im[..., None] * s5_b_im
    bb_im = co_re[..., None] * s5_b_im + co_im[..., None] * s5_b_re
    bre = bd(jnp.swapaxes(bb_re, 1, 2))
    bim = bd(jnp.swapaxes(bb_im, 1, 2))
    bbd = jnp.stack([bre.reshape(D_S5, ns, LANES), bim.reshape(D_S5, ns, LANES)], axis=2)
    bbd = bbd.reshape(D_S5, 2 * S5_P).astype(BF16)
    cre = bd(jnp.swapaxes(s5_c_re, 1, 2))
    cim = bd(jnp.swapaxes(s5_c_im, 1, 2))
    cbd = jnp.concatenate([cre, -cim], axis=0).astype(BF16)
    return dict(cw=lru_conv_w, cb=lru_conv_b.reshape(1, D_LRU), wg=wg, bg=bg, c8=c8,
                lbr=lb_re.reshape(ns, 1, LANES), lbi=lb_im.reshape(ns, 1, LANES),
                bbd=bbd, cbd=cbd, d=s5_d.reshape(1, D_S5),
                wglu=s5_w_glu.astype(BF16), bglu=s5_b_glu.reshape(1, D_S5))


def _moba_prompt_kernel(qt_ref, krm_ref, kt_ref, vt_ref, o_ref, km_s, vb_s, bias_s, sc_s, *, s, nblk, grp):
    qi = pl.program_id(2)
    tq = MOBA_BLOCK
    hd = HEAD_DIM
    gk = grp * tq

    @pl.when(qi == 0)
    def _():
        r = lax.broadcasted_iota(jnp.int32, (nblk, s), 1) // MOBA_BLOCK
        n = lax.broadcasted_iota(jnp.int32, (nblk, s), 0)
        pm = jnp.where(r == n, 1.0 / MOBA_BLOCK, 0.0).astype(F32)
        for hh in range(2):
            km_s[hh] = lax.dot_general(pm, kt_ref[0, hh * hd:(hh + 1) * hd, :], NT_DIMS,
                                       precision=HIGHEST, preferred_element_type=F32)
        ones_row = (lax.broadcasted_iota(jnp.int32, (VB_ROWS - hd, gk), 0) == 0).astype(BF16)
        for g in range(nblk // grp):
            for hh in range(2):
                vb_s[g, hh, 0:hd, :] = vt_ref[0, hh * hd:(hh + 1) * hd, g * gk:(g + 1) * gk].astype(BF16)
                vb_s[g, hh, hd:VB_ROWS, :] = ones_row

    qt = qt_ref[0]
    qs = (qt * (hd ** -0.5 * LOG2E)).astype(BF16)
    feat = lax.broadcasted_iota(jnp.int32, (2 * hd, tq), 0)
    blk_iota = lax.broadcasted_iota(jnp.int32, (nblk, tq), 0)
    past = blk_iota < qi
    ws = []
    for hh in range(2):
        ws.append(jnp.where((feat >= hh * hd) & (feat < (hh + 1) * hd), qs, jnp.zeros_like(qs)))
        sb = jnp.dot(km_s[hh], qt[hh * hd:(hh + 1) * hd, :], precision=HIGHEST,
                     preferred_element_type=F32)
        sb = jnp.where(past, sb, -jnp.inf)
        cnt = jnp.zeros((nblk, tq), jnp.int32)
        for i in range(nblk):
            ri = sb[i:i + 1, :]
            beats = (ri > sb) | ((ri == sb) & (i < blk_iota))
            cnt = cnt + beats.astype(jnp.int32)
        sel = (cnt < MOBA_TOPK) & past
        bias_s[hh] = jnp.where(sel | (blk_iota == qi), 0.0, NEG).astype(F32)

    def scores(g, hh):
        st = pl.multiple_of(g * gk, gk)
        sc = jnp.dot(krm_ref[0, pl.ds(st, gk), :], ws[hh], preferred_element_type=F32)
        return jnp.concatenate(
            [sc[i * tq:(i + 1) * tq] + bias_s[hh, pl.ds(g * grp + i, 1), :] for i in range(grp)], axis=0)

    def pass1(g, ms):
        st = pl.multiple_of(g * gk, gk)
        out = []
        for hh in range(2):
            sc = scores(g, hh)
            sc_s[hh, pl.ds(st, gk), :] = sc
            out.append(jnp.maximum(ms[hh], jnp.max(sc, axis=0, keepdims=True)))
        return tuple(out)

    g_own = qi // grp
    ms = lax.fori_loop(0, g_own, pass1, tuple(jnp.full((1, tq), NEG, F32) for _ in range(2)))
    key = lax.broadcasted_iota(jnp.int32, (gk, tq), 0)
    qry = lax.broadcasted_iota(jnp.int32, (gk, tq), 1)
    causal = key <= qry + (qi * tq - g_own * gk)
    st_own = pl.multiple_of(g_own * gk, gk)
    ms = list(ms)
    for hh in range(2):
        sc = jnp.where(causal, scores(g_own, hh), NEG)
        sc_s[hh, pl.ds(st_own, gk), :] = sc
        ms[hh] = jnp.maximum(ms[hh], jnp.max(sc, axis=0, keepdims=True))

    def pass2(g, accs):
        st = pl.multiple_of(g * gk, gk)
        out = []
        for hh in range(2):
            p = jnp.exp2(sc_s[hh, pl.ds(st, gk), :] - ms[hh])
            out.append(accs[hh] + jnp.dot(vb_s[g, hh], p.astype(BF16), preferred_element_type=F32))
        return tuple(out)

    accs = lax.fori_loop(0, g_own + 1, pass2, tuple(jnp.zeros((VB_ROWS, tq), F32) for _ in range(2)))
    out_t = jnp.concatenate([acc[0:hd] / acc[hd:hd + 1] for acc in accs], axis=0)
    o_ref[0] = jnp.transpose(out_t)


def _moba_prompt(qt, krm, kt, vt):
    nb, _, s = qt.shape
    nblk = s // MOBA_BLOCK
    tq = MOBA_BLOCK
    hp = 2 * HEAD_DIM
    grp = 4
    return pl.pallas_call(
        functools.partial(_moba_prompt_kernel, s=s, nblk=nblk, grp=grp),
        grid=(nb, N_HEADS // 2, nblk),
        in_specs=[pl.BlockSpec((1, hp, tq), lambda b, h, i: (b, h, i)),
                  pl.BlockSpec((1, s, hp), lambda b, h, i: (b, 0, h)),
                  pl.BlockSpec((1, hp, s), lambda b, h, i: (b, h, 0)),
                  pl.BlockSpec((1, hp, s), lambda b, h, i: (b, h, 0))],
        out_specs=pl.BlockSpec((1, tq, hp), lambda b, h, i: (b, i, h)),
        out_shape=jax.ShapeDtypeStruct((nb, s, D_ATT), F32),
        scratch_shapes=[pltpu.VMEM((2, nblk, HEAD_DIM), F32),
                        pltpu.VMEM((nblk // grp, 2, VB_ROWS, grp * tq), BF16),
                        pltpu.VMEM((2, nblk, tq), F32),
                        pltpu.VMEM((2, s, tq), F32)],
        compiler_params=_cparams(("arbitrary", "arbitrary", "arbitrary")),
        name="moba_prompt",
    )(qt, krm, kt, vt)


PAGES_PER_BLOCK = MOBA_BLOCK // PAGE_SIZE
N_PAGES = PAST_LEN // PAGE_SIZE
N_FULL = PAST_LEN // MOBA_BLOCK
PG = 16


def _page_copy(cache_ref, pt_ref, buf, sem, layer, step, slot, j):
    pid = pt_ref[step * PG + j]
    return pltpu.make_async_copy(cache_ref.at[layer, pid], buf.at[slot, j], sem.at[slot, j])


def _ksum_kernel(pt_ref, cache_ref, o_ref, buf, sem, *, layer, nsteps, steps_per_b):
    n = pl.program_id(0)
    slot = n % 2
    g = n % steps_per_b
    bpg = PG // PAGES_PER_BLOCK

    def start(step, sl):
        for j in range(PG):
            _page_copy(cache_ref, pt_ref, buf, sem, layer, step, sl, j).start()

    @pl.when(n == 0)
    def _():
        start(0, 0)

    @pl.when(n + 1 < nsteps)
    def _():
        start(n + 1, 1 - slot)

    @pl.when(g == 0)
    def _():
        o_ref[...] = jnp.zeros(o_ref.shape, F32)

    for j in range(PG):
        _page_copy(cache_ref, pt_ref, buf, sem, layer, n, slot, j).wait()
    lane = lax.broadcasted_iota(jnp.int32, (HEAD_DIM, LANES), 1)
    for hh in range(N_HEADS):
        acc = o_ref[0, hh]
        for blk in range(bpg):
            tile = buf[slot, PAGES_PER_BLOCK * blk, hh]
            for pp in range(1, PAGES_PER_BLOCK):
                tile = tile + buf[slot, PAGES_PER_BLOCK * blk + pp, hh]
            acc = jnp.where(lane == g * bpg + blk, jnp.sum(tile, axis=1, keepdims=True), acc)
        o_ref[0, hh] = acc


def _block_ksum(cache_kt, pt_flat, layer, nbat):
    steps_per_b = N_PAGES // PG
    nsteps = nbat * steps_per_b
    return pl.pallas_call(
        functools.partial(_ksum_kernel, layer=layer, nsteps=nsteps, steps_per_b=steps_per_b),
        grid_spec=pltpu.PrefetchScalarGridSpec(
            num_scalar_prefetch=1,
            grid=(nsteps,),
            in_specs=[pl.BlockSpec(memory_space=pl.ANY)],
            out_specs=pl.BlockSpec((1, N_HEADS, HEAD_DIM, LANES),
                                   lambda n, pt: (n // steps_per_b, 0, 0, 0)),
            scratch_shapes=[pltpu.VMEM((2, PG, N_HEADS, HEAD_DIM, PAGE_SIZE), F32),
                            pltpu.SemaphoreType.DMA((2, PG))]),
        out_shape=jax.ShapeDtypeStruct((nbat, N_HEADS, HEAD_DIM, LANES), F32),
        compiler_params=_cparams(("arbitrary",)),
        name="block_ksum",
    )(pt_flat, cache_kt)


def _topk_kernel(q_ref, ks_ref, o_ref):
    nq = q_ref.shape[2]
    lane = lax.broadcasted_iota(jnp.int32, (nq, LANES), 1)
    for hh in range(N_HEADS):
        km = ks_ref[0, hh] * (1.0 / MOBA_BLOCK)
        sb = jnp.dot(q_ref[0, hh], km, precision=HIGHEST, preferred_element_type=F32)
        sb = jnp.where(lane < N_FULL, sb, -jnp.inf)
        out = jnp.zeros((nq, LANES), jnp.int32)
        for r in range(MOBA_TOPK):
            mx = jnp.max(sb, axis=1, keepdims=True)
            idx = jnp.min(jnp.where(sb == mx, lane, LANES), axis=1, keepdims=True)
            out = jnp.where(lane == r, idx, out)
            sb = jnp.where(lane == idx, -jnp.inf, sb)
        o_ref[0, hh] = out


def _sample_topk(q_pad, ksum_t):
    nbat, nh, nq, hd = q_pad.shape
    return pl.pallas_call(
        _topk_kernel,
        grid=(nbat,),
        in_specs=[pl.BlockSpec((1, nh, nq, hd), lambda b: (b, 0, 0, 0)),
                  pl.BlockSpec((1, nh, hd, LANES), lambda b: (b, 0, 0, 0))],
        out_specs=pl.BlockSpec((1, nh, nq, LANES), lambda b: (b, 0, 0, 0)),
        out_shape=jax.ShapeDtypeStruct((nbat, nh, nq, LANES), jnp.int32),
        compiler_params=_cparams(("arbitrary",)),
        name="sample_topk",
    )(q_pad, ksum_t)


def _sel_copy(cache_ref, idx_ref, pt_ref, buf, sem, layer, step, slot, t, r, pp, nt):
    b = step // N_HEADS
    hh = step % N_HEADS
    blk = idx_ref[(step * nt + t) * MOBA_TOPK + r]
    pid = pt_ref[b * N_PAGES + blk * PAGES_PER_BLOCK + pp]
    j = (t * MOBA_TOPK + r) * PAGES_PER_BLOCK + pp
    return pltpu.make_async_copy(cache_ref.at[layer, pid, hh], buf.at[slot, j], sem.at[slot, j])


def _moba_sample_kernel(idx_ref, pt_ref, q_ref, kn_ref, vn_ref, ck_ref, cv_ref, o_ref,
                        kbuf, vbuf, ksem, vsem, *, layer, nsteps, nt):
    n = pl.program_id(0)
    slot = n % 2
    b = n // N_HEADS
    nsel = MOBA_TOPK * PAGES_PER_BLOCK

    def copies(step, sl):
        out = []
        for t in range(nt):
            for r in range(MOBA_TOPK):
                for pp in range(PAGES_PER_BLOCK):
                    out.append(_sel_copy(ck_ref, idx_ref, pt_ref, kbuf, ksem, layer, step, sl, t, r, pp, nt))
                    out.append(_sel_copy(cv_ref, idx_ref, pt_ref, vbuf, vsem, layer, step, sl, t, r, pp, nt))
        return out

    @pl.when(n == 0)
    def _():
        for cp in copies(0, 0):
            cp.start()

    @pl.when(n + 1 < nsteps)
    def _():
        for cp in copies(n + 1, 1 - slot):
            cp.start()

    for cp in copies(n, slot):
        cp.wait()

    qs = (q_ref[0, 0] * (HEAD_DIM ** -0.5)).astype(BF16)
    nq = qs.shape[0]
    nkeys = nsel * PAGE_SIZE
    qrow = lax.broadcasted_iota(jnp.int32, (nq, nkeys), 0)
    scores = []
    for t in range(nt):
        kt = jnp.concatenate([kbuf[slot, t * nsel + i] for i in range(nsel)], axis=1).astype(BF16)
        st = jnp.dot(qs, kt, preferred_element_type=F32)
        scores.append(jnp.where(qrow == t, st, -jnp.inf))
    ntok = kn_ref.shape[2]
    so = jnp.dot(qs, kn_ref[0].astype(BF16), preferred_element_type=F32)
    orow = lax.broadcasted_iota(jnp.int32, (nq, ntok), 0)
    opos = lax.broadcasted_iota(jnp.int32, (nq, ntok), 1) - b * nt
    so = jnp.where((opos >= 0) & (opos <= orow), so, -jnp.inf)
    m = jnp.max(so, axis=1, keepdims=True)
    for st in scores:
        m = jnp.maximum(m, jnp.max(st, axis=1, keepdims=True))
    po = jnp.exp(so - m)
    den = jnp.sum(po, axis=1, keepdims=True)
    acc = lax.dot_general(po.astype(BF16), vn_ref[0].astype(BF16), NT_DIMS, preferred_element_type=F32)
    for t in range(nt):
        pt_ = jnp.exp(scores[t] - m)
        den = den + jnp.sum(pt_, axis=1, keepdims=True)
        vt = jnp.concatenate([vbuf[slot, t * nsel + i] for i in range(nsel)], axis=1).astype(BF16)
        acc = acc + lax.dot_general(pt_.astype(BF16), vt, NT_DIMS, preferred_element_type=F32)
    o_ref[0, 0] = acc / den


def _moba_sample(q_pad, kt_new, vt_new, cache_kt, cache_vt, idx_flat, pt_flat, layer, nt):
    nbat, nh, nq, hd = q_pad.shape
    ntok = kt_new.shape[1]
    nsteps = nbat * nh
    nslab = nt * MOBA_TOPK * PAGES_PER_BLOCK
    qtok = pl.BlockSpec((1, 1, nq, hd), lambda n, idx, pt: (n // nh, n % nh, 0, 0))
    new = pl.BlockSpec((1, hd, ntok), lambda n, idx, pt: (n % nh, 0, 0))
    return pl.pallas_call(
        functools.partial(_moba_sample_kernel, layer=layer, nsteps=nsteps, nt=nt),
        grid_spec=pltpu.PrefetchScalarGridSpec(
            num_scalar_prefetch=2,
            grid=(nsteps,),
            in_specs=[qtok, new, new, pl.BlockSpec(memory_space=pl.ANY), pl.BlockSpec(memory_space=pl.ANY)],
            out_specs=qtok,
            scratch_shapes=[pltpu.VMEM((2, nslab, hd, PAGE_SIZE), F32),
                            pltpu.VMEM((2, nslab, hd, PAGE_SIZE), F32),
                            pltpu.SemaphoreType.DMA((2, nslab)),
                            pltpu.SemaphoreType.DMA((2, nslab))]),
        out_shape=jax.ShapeDtypeStruct((nbat, nh, nq, hd), F32),
        compiler_params=_cparams(("arbitrary",)),
        name="moba_sample",
    )(idx_flat, pt_flat, q_pad, kt_new.reshape(nh, hd, ntok), vt_new.reshape(nh, hd, ntok),
      cache_kt, cache_vt)


def _rope_tables(pos):
    inv = jnp.power(ROPE_THETA, -jnp.arange(HALF, dtype=F32) / HALF)
    ang = pos.astype(F32)[:, None] * inv[None, :]
    cos, sin = jnp.cos(ang), jnp.sin(ang)
    reps = LANES // HEAD_DIM
    return (jnp.tile(jnp.concatenate([cos, cos], axis=1), (1, reps)),
            jnp.tile(jnp.concatenate([-sin, sin], axis=1), (1, reps)),
            cos.T, sin.T)


def _split_w_in(w):
    a_x, a_g, q, k, v, b_g, s_x, s_g = jnp.split(
        w, (D_LRU, 2 * D_LRU, 2 * D_LRU + D_ATT, 2 * D_LRU + 2 * D_ATT, 2 * D_LRU + 3 * D_ATT,
            2 * D_LRU + 4 * D_ATT, 2 * D_LRU + 4 * D_ATT + D_S5), axis=1)
    w_row = jnp.concatenate([a_x, s_x, a_g, b_g, s_g, k], axis=1).astype(BF16)
    w_t = jnp.concatenate([q, k, v], axis=1).T.astype(BF16)
    return w_row, w_t


def kernel(x_prompt, x_sample, cache_k, cache_v, state_lru_h, state_lru_conv, state_s5_re, state_s5_im,
           page_table, c_prompt, c_sample, norm_g, w_ada, b_ada, w_in, w_out, lru_conv_w, lru_conv_b,
           lru_w_r, lru_b_r, lru_w_i, lru_b_i, lru_lam, s5_a_re, s5_a_im, s5_log_dt, s5_b_re, s5_b_im,
           s5_c_re, s5_c_im, s5_d, s5_w_glu, s5_b_glu, final_g):
    nbp, seq, _ = x_prompt.shape
    nbs, nts, _ = x_sample.shape
    ntok_s = nbs * nts

    rope_p = _rope_tables(jnp.arange(seq, dtype=jnp.int32))
    rope_s = _rope_tables(PAST_LEN + (jnp.arange(ntok_s, dtype=jnp.int32) % nts))

    n_c = nbp + nbs
    c_all = jnp.concatenate([c_prompt, c_sample, jnp.zeros((-n_c % 8, D_MODEL), F32)], axis=0)
    mod = _modulation(c_all, w_ada, b_ada)

    cache_kt = jnp.swapaxes(cache_k, 3, 4)
    cache_vt = jnp.swapaxes(cache_v, 3, 4)
    pt_flat = page_table.reshape(-1)
    xp = x_prompt
    xs = x_sample.reshape(1, ntok_s, D_MODEL)
    outs_p = [[] for _ in range(6)]
    outs_s = [[] for _ in range(6)]
    for l in range(DEPTH):
        last = l == DEPTH - 1
        w_row, w_t = _split_w_in(w_in[l])
        w_o = w_out[l].astype(BF16)
        pr = _rec_params(lru_conv_w[l], lru_conv_b[l], lru_w_r[l], lru_b_r[l], lru_w_i[l], lru_b_i[l],
                         lru_lam[l], s5_a_re[l], s5_a_im[l], s5_log_dt[l], s5_b_re[l], s5_b_im[l],
                         s5_c_re[l], s5_c_im[l], s5_d[l], s5_w_glu[l], s5_b_glu[l])
        shift, scale, gate = jnp.split(mod[l], 3, axis=-1)

        mp = lambda a: a[:nbp].reshape(nbp, 1, D_MODEL)
        u, gates, krm, qt, kt, vt = _inproj(xp, mp(scale), mp(shift), norm_g[l], w_row, w_t, rope_p, tm=512)
        as_out, h_last, cbuf, s_re, s_im = _rec_prompt(u, pr, tc=256)
        att = _moba_prompt(qt, krm, kt, vt)
        xp = _outproj(xp, as_out, att, gates, mp(gate), w_o, final_g, tm=512, final=last)
        for lst, val in zip(outs_p, (kt.reshape(nbp, N_HEADS, HEAD_DIM, seq), vt.reshape(nbp, N_HEADS, HEAD_DIM, seq),
                                     h_last, cbuf, s_re.reshape(nbp, S5_GROUPS, S5_STATE),
                                     s_im.reshape(nbp, S5_GROUPS, S5_STATE))):
            lst.append(val)

        ms = lambda a: jnp.repeat(a[nbp:n_c], nts, axis=0).reshape(1, ntok_s, D_MODEL)
        u, gates, _, qt, kt, vt = _inproj(xs, ms(scale), ms(shift), norm_g[l], w_row, w_t, rope_s, tm=ntok_s)
        to_heads = lambda a: a.reshape(N_HEADS, HEAD_DIM, nbs, nts).transpose(2, 0, 3, 1)
        q, k, v = to_heads(qt), to_heads(kt), to_heads(vt)
        u_t = u.reshape(nbs, nts, U_COLS).transpose(1, 0, 2)
        as_t, h_last, cbuf_t, s_re, s_im = _rec_sample(
            u_t, state_lru_h[l], state_lru_conv[l].transpose(1, 0, 2),
            state_s5_re[l].reshape(nbs, S5_P), state_s5_im[l].reshape(nbs, S5_P), pr)
        as_out = as_t.transpose(1, 0, 2).reshape(1, ntok_s, U_COLS)
        ksum_t = _block_ksum(cache_kt, pt_flat, l, nbs)
        q_pad = jnp.pad(q, ((0, 0), (0, 0), (0, -nts % 8), (0, 0)))
        idx = _sample_topk(q_pad, ksum_t)[:, :, :nts, :MOBA_TOPK].reshape(-1)
        att = _moba_sample(q_pad, kt[0], vt[0], cache_kt, cache_vt, idx, pt_flat, l, nts)[:, :, :nts]
        att = att.transpose(0, 2, 1, 3).reshape(1, ntok_s, D_ATT)
        xs = _outproj(xs, as_out, att, gates, ms(gate), w_o, final_g, tm=ntok_s, final=last)
        for lst, val in zip(outs_s, (k, v, h_last, cbuf_t.transpose(1, 0, 2),
                                     s_re.reshape(nbs, S5_GROUPS, S5_STATE),
                                     s_im.reshape(nbs, S5_GROUPS, S5_STATE))):
            lst.append(val)

    k_p, v_p = (jnp.swapaxes(jnp.stack(o), 3, 4) for o in outs_p[:2])
    return (xp, xs.reshape(nbs, nts, D_MODEL), k_p, v_p,
            *[jnp.stack(o) for o in outs_p[2:]], *[jnp.stack(o) for o in outs_s])
```

```python
import functools

import jax
import jax.numpy as jnp
from jax import lax
from jax.experimental import pallas as pl
from jax.experimental.pallas import tpu as pltpu

F32 = jnp.float32
BF16 = jnp.bfloat16
HIGHEST = lax.Precision.HIGHEST

D_MODEL = 1024
DEPTH = 2
PAST_LEN = 16384
PAGE_SIZE = 128
D_LRU = 256
LRU_BLOCKS = 4
LRU_BW = D_LRU // LRU_BLOCKS
CONV_W = 4
LRU_C = 8.0
N_HEADS = 8
HEAD_DIM = 64
HALF = HEAD_DIM // 2
D_ATT = N_HEADS * HEAD_DIM
MOBA_BLOCK = 256
MOBA_TOPK = 3
ROPE_THETA = 10000.0
D_S5 = 256
S5_GROUP = 16
S5_GROUPS = D_S5 // S5_GROUP
S5_STATE = 64
S5_P = S5_GROUPS * S5_STATE
D_MIX = D_LRU + D_ATT + D_S5
EPS = 1e-6

LANES = 128
BF16_ROWS = 16
NEG = -1e30
LOG2E = 1.4426950408889634
VB_ROWS = HEAD_DIM + BF16_ROWS

U_COLS = D_LRU + D_S5
G_COLS = D_MIX
OFF_U = 0
OFF_G = OFF_U + U_COLS
OFF_KR = OFF_G + G_COLS

VMEM_LIMIT = 56 * 1024 * 1024

NT_DIMS = (((1,), (1,)), ((), ()))


def _cparams(sem):
    return pltpu.CompilerParams(dimension_semantics=sem, vmem_limit_bytes=VMEM_LIMIT)


def _silu(x):
    return x * jax.nn.sigmoid(x)


def _mod_kernel(c_ref, w_ref, b_ref, o_ref):
    o_ref[0] = jnp.dot(_silu(c_ref[...]), w_ref[0], precision=HIGHEST,
                       preferred_element_type=F32) + b_ref[0]


def _modulation(c_all, w_ada, b_ada):
    n = c_all.shape[0]
    tn = D_MODEL
    return pl.pallas_call(
        _mod_kernel,
        grid=(DEPTH, 3 * D_MODEL // tn),
        in_specs=[pl.BlockSpec((n, D_MODEL), lambda l, j: (0, 0)),
                  pl.BlockSpec((1, D_MODEL, tn), lambda l, j: (l, 0, j)),
                  pl.BlockSpec((1, 1, tn), lambda l, j: (l, 0, j))],
        out_specs=pl.BlockSpec((1, n, tn), lambda l, j: (l, 0, j)),
        out_shape=jax.ShapeDtypeStruct((DEPTH, n, 3 * D_MODEL), F32),
        compiler_params=_cparams(("arbitrary", "arbitrary")),
        name="modulation",
    )(c_all, w_ada, b_ada.reshape(DEPTH, 1, 3 * D_MODEL))


def _inproj_kernel(x_ref, sc_ref, sh_ref, g_ref, w_ref, wt_ref, cos_ref, sin_ref, cost_ref, sint_ref,
                   u_ref, gate_ref, krm_ref, qt_ref, kt_ref, vt_ref):
    x = x_ref[0]
    h = x * lax.rsqrt(jnp.mean(x * x, axis=-1, keepdims=True) + EPS) * g_ref[...]
    h = h * (1.0 + sc_ref[0]) + sh_ref[0]
    hb = h.astype(BF16)

    def proj(lo, n):
        return jnp.dot(hb, w_ref[:, lo:lo + n], preferred_element_type=F32)

    u_ref[0] = proj(OFF_U, U_COLS)
    gate_ref[0] = _silu(proj(OFF_G, G_COLS)).astype(gate_ref.dtype)

    cos = cos_ref[...]
    sin = sin_ref[...]
    lane = lax.broadcasted_iota(jnp.int32, cos.shape, 1)
    first_half = (lane & (HEAD_DIM - 1)) < HALF
    p = proj(OFF_KR, D_ATT)
    for j in range(D_ATT // LANES):
        xs = p[:, j * LANES:(j + 1) * LANES]
        partner = jnp.where(first_half, pltpu.roll(xs, LANES - HALF, 1), pltpu.roll(xs, HALF, 1))
        krm_ref[0, :, j * LANES:(j + 1) * LANES] = (xs * cos + partner * sin).astype(BF16)

    cost = cost_ref[...]
    sint = sint_ref[...]
    for n, (ref, use_rope) in enumerate(((qt_ref, True), (kt_ref, True), (vt_ref, False))):
        pt = lax.dot_general(wt_ref[n * D_ATT:(n + 1) * D_ATT, :], hb, NT_DIMS,
                             preferred_element_type=F32)
        if not use_rope:
            ref[0] = pt
            continue
        for hh in range(N_HEADS):
            r0 = hh * HEAD_DIM
            x1 = pt[r0:r0 + HALF]
            x2 = pt[r0 + HALF:r0 + HEAD_DIM]
            ref[0, r0:r0 + HALF, :] = x1 * cost - x2 * sint
            ref[0, r0 + HALF:r0 + HEAD_DIM, :] = x2 * cost + x1 * sint


def _inproj(x, scale, shift, g, w_row, w_t, rope, tm):
    nb, s, _ = x.shape
    cos_r, sin_r, cos_t, sin_t = rope
    rm = scale.shape[1]
    mod_block = (1, tm, D_MODEL) if rm == s else (1, 1, D_MODEL)
    mod_map = (lambda b, i: (b, i, 0)) if rm == s else (lambda b, i: (b, 0, 0))
    fm_spec = pl.BlockSpec((1, D_ATT, tm), lambda b, i: (b, 0, i))
    fm_shape = jax.ShapeDtypeStruct((nb, D_ATT, s), F32)
    return pl.pallas_call(
        _inproj_kernel,
        grid=(nb, s // tm),
        in_specs=[pl.BlockSpec((1, tm, D_MODEL), lambda b, i: (b, i, 0)),
                  pl.BlockSpec(mod_block, mod_map),
                  pl.BlockSpec(mod_block, mod_map),
                  pl.BlockSpec((1, D_MODEL), lambda b, i: (0, 0)),
                  pl.BlockSpec(w_row.shape, lambda b, i: (0, 0)),
                  pl.BlockSpec(w_t.shape, lambda b, i: (0, 0)),
                  pl.BlockSpec((tm, LANES), lambda b, i: (i, 0)),
                  pl.BlockSpec((tm, LANES), lambda b, i: (i, 0)),
                  pl.BlockSpec((HALF, tm), lambda b, i: (0, i)),
                  pl.BlockSpec((HALF, tm), lambda b, i: (0, i))],
        out_specs=[pl.BlockSpec((1, tm, U_COLS), lambda b, i: (b, i, 0)),
                   pl.BlockSpec((1, tm, G_COLS), lambda b, i: (b, i, 0)),
                   pl.BlockSpec((1, tm, D_ATT), lambda b, i: (b, i, 0)),
                   fm_spec, fm_spec, fm_spec],
        out_shape=[jax.ShapeDtypeStruct((nb, s, U_COLS), F32),
                   jax.ShapeDtypeStruct((nb, s, G_COLS), BF16),
                   jax.ShapeDtypeStruct((nb, s, D_ATT), BF16),
                   fm_shape, fm_shape, fm_shape],
        compiler_params=_cparams(("arbitrary", "arbitrary")),
        name="inproj",
    )(x, scale, shift, g.reshape(1, D_MODEL), w_row, w_t, cos_r, sin_r, cos_t, sin_t)


def _outproj_kernel(x_ref, as_ref, att_ref, gate_ref, gm_ref, w_ref, fg_ref, o_ref, *, final):
    g = gate_ref[0].astype(F32)
    a_s = as_ref[0].astype(F32)
    mixed = jnp.concatenate([a_s[:, :D_LRU] * g[:, :D_LRU],
                             att_ref[0].astype(F32) * g[:, D_LRU:D_LRU + D_ATT],
                             a_s[:, D_LRU:] * g[:, D_LRU + D_ATT:]], axis=1).astype(BF16)
    y = jnp.dot(mixed, w_ref[...], preferred_element_type=F32)
    xn = x_ref[0] + gm_ref[0] * y
    if final:
        xn = xn * lax.rsqrt(jnp.mean(xn * xn, axis=-1, keepdims=True) + EPS) * fg_ref[...]
    o_ref[0] = xn


def _outproj(x, as_out, att, gates, gate_mod, w_out, final_g, tm, final):
    nb, s, _ = x.shape
    rm = gate_mod.shape[1]
    mod_block = (1, tm, D_MODEL) if rm == s else (1, 1, D_MODEL)
    mod_map = (lambda b, i: (b, i, 0)) if rm == s else (lambda b, i: (b, 0, 0))
    row = lambda n: pl.BlockSpec((1, tm, n), lambda b, i: (b, i, 0))
    return pl.pallas_call(
        functools.partial(_outproj_kernel, final=final),
        grid=(nb, s // tm),
        in_specs=[row(D_MODEL), row(U_COLS), row(D_ATT), row(G_COLS),
                  pl.BlockSpec(mod_block, mod_map),
                  pl.BlockSpec((D_MIX, D_MODEL), lambda b, i: (0, 0)),
                  pl.BlockSpec((1, D_MODEL), lambda b, i: (0, 0))],
        out_specs=row(D_MODEL),
        out_shape=jax.ShapeDtypeStruct((nb, s, D_MODEL), F32),
        compiler_params=_cparams(("arbitrary", "arbitrary")),
        name="outproj",
    )(x, as_out, att, gates, gate_mod, w_out, final_g.reshape(1, D_MODEL))


def _lru_gates(xc, wg_ref, bg_ref, c8_ref):
    gates = jnp.dot(xc.astype(BF16), wg_ref[...], preferred_element_type=F32) + bg_ref[...]
    r = jax.nn.sigmoid(gates[:, :D_LRU])
    i = jax.nn.sigmoid(gates[:, D_LRU:])
    log_a = r * c8_ref[...]
    a = jnp.exp(log_a)
    mult = jnp.sqrt(1.0 - jnp.exp(2.0 * log_a))
    return a, mult * i * xc


def _s5_out(x_cat_bf16, sx, cbd_ref, d_ref, wglu_ref, bglu_ref):
    y = jnp.dot(x_cat_bf16, cbd_ref[...], preferred_element_type=F32) + d_ref[...] * sx
    z = jax.nn.gelu(y)
    return z * jax.nn.sigmoid(jnp.dot(z.astype(BF16), wglu_ref[...], preferred_element_type=F32)
                              + bglu_ref[...])


def _rec_prompt_kernel(u_ref, cw_ref, cb_ref, wg_ref, bg_ref, c8_ref, lbr_ref, lbi_ref,
                       bbd_ref, cbd_ref, d_ref, wglu_ref, bglu_ref,
                       o_ref, hl_ref, cbuf_ref, sre_ref, sim_ref,
                       ext, a_s, b_s, h_s, bu_s, x_s, h_st, x_st, *, nb, tc, pitch):
    c = pl.program_id(0)
    nl = D_LRU // LANES
    ns = S5_P // LANES

    @pl.when(c == 0)
    def _():
        ext[:, 0:8, :] = jnp.zeros((nb, 8, D_LRU), F32)
        h_st[...] = jnp.zeros(h_st.shape, F32)
        x_st[...] = jnp.zeros(x_st.shape, F32)

    for b in range(nb):
        r0 = b * pitch
        ext[b, 8:8 + tc, :] = u_ref[b, :, 0:D_LRU]
        xc = cb_ref[...]
        for j in range(CONV_W):
            xc = xc + ext[b, 8 - (CONV_W - 1) + j:8 - (CONV_W - 1) + j + tc, :] * cw_ref[j:j + 1, :]
        ext[b, 8 - (CONV_W - 1):8, :] = ext[b, 8 + tc - (CONV_W - 1):8 + tc, :]
        a, bb = _lru_gates(xc, wg_ref, bg_ref, c8_ref)
        for j in range(nl):
            a_s[j, r0:r0 + tc, :] = a[:, j * LANES:(j + 1) * LANES]
            b_s[j, r0:r0 + tc, :] = bb[:, j * LANES:(j + 1) * LANES]
        sxb = u_ref[b, :, D_LRU:U_COLS].astype(BF16)
        for j in range(ns):
            bu = jnp.dot(sxb, bbd_ref[:, 2 * j * LANES:(2 * j + 2) * LANES], preferred_element_type=F32)
            bu_s[j, r0:r0 + tc, :] = bu[:, :LANES]
            bu_s[ns + j, r0:r0 + tc, :] = bu[:, LANES:]

    def step(t, carry):
        h, xr, xi = carry
        rows = pl.ds(t, nb, stride=pitch)
        h_new, xr_new, xi_new = [], [], []
        for j in range(nl):
            hj = a_s[j, rows, :] * h[j] + b_s[j, rows, :]
            h_s[j, rows, :] = hj
            h_new.append(hj)
        for j in range(ns):
            lr = lbr_ref[j]
            li = lbi_ref[j]
            nr = lr * xr[j] - li * xi[j] + bu_s[j, rows, :]
            ni = lr * xi[j] + li * xr[j] + bu_s[ns + j, rows, :]
            x_s[j, rows, :] = nr
            x_s[ns + j, rows, :] = ni
            xr_new.append(nr)
            xi_new.append(ni)
        return tuple(h_new), tuple(xr_new), tuple(xi_new)

    init = (tuple(h_st[j] for j in range(nl)),
            tuple(x_st[j] for j in range(ns)),
            tuple(x_st[ns + j] for j in range(ns)))
    h, xr, xi = lax.fori_loop(0, tc, step, init)
    for j in range(nl):
        h_st[j] = h[j]
        hl_ref[:, j * LANES:(j + 1) * LANES] = h[j]
    for j in range(ns):
        x_st[j] = xr[j]
        x_st[ns + j] = xi[j]
        sre_ref[:, j * LANES:(j + 1) * LANES] = xr[j]
        sim_ref[:, j * LANES:(j + 1) * LANES] = xi[j]

    for b in range(nb):
        r0 = b * pitch
        for j in range(nl):
            o_ref[b, :, j * LANES:(j + 1) * LANES] = h_s[j, r0:r0 + tc, :].astype(o_ref.dtype)
        x_cat = jnp.concatenate([x_s[j, r0:r0 + tc, :].astype(BF16) for j in range(2 * ns)], axis=1)
        o_ref[b, :, D_LRU:U_COLS] = _s5_out(x_cat, u_ref[b, :, D_LRU:U_COLS], cbd_ref, d_ref,
                                            wglu_ref, bglu_ref).astype(o_ref.dtype)
        cbuf_ref[b] = ext[b, 8 - (CONV_W - 1):8, :]


def _rec_prompt(u, pr, tc):
    nb, s, _ = u.shape
    pitch = tc + 8
    nl = D_LRU // LANES
    ns = S5_P // LANES
    full = lambda a: pl.BlockSpec(a.shape, lambda c, _n=a.ndim: (0,) * _n)
    params = [pr["cw"], pr["cb"], pr["wg"], pr["bg"], pr["c8"], pr["lbr"], pr["lbi"],
              pr["bbd"], pr["cbd"], pr["d"], pr["wglu"], pr["bglu"]]
    st = lambda *shape: pl.BlockSpec(shape, lambda c, _n=len(shape): (0,) * _n)
    return pl.pallas_call(
        functools.partial(_rec_prompt_kernel, nb=nb, tc=tc, pitch=pitch),
        grid=(s // tc,),
        in_specs=[pl.BlockSpec((nb, tc, U_COLS), lambda c: (0, c, 0))] + [full(a) for a in params],
        out_specs=[pl.BlockSpec((nb, tc, U_COLS), lambda c: (0, c, 0)),
                   st(nb, D_LRU), st(nb, CONV_W - 1, D_LRU), st(nb, S5_P), st(nb, S5_P)],
        out_shape=[jax.ShapeDtypeStruct((nb, s, U_COLS), BF16),
                   jax.ShapeDtypeStruct((nb, D_LRU), F32),
                   jax.ShapeDtypeStruct((nb, CONV_W - 1, D_LRU), F32),
                   jax.ShapeDtypeStruct((nb, S5_P), F32),
                   jax.ShapeDtypeStruct((nb, S5_P), F32)],
        scratch_shapes=[pltpu.VMEM((nb, tc + 8, D_LRU), F32),
                        pltpu.VMEM((nl, nb * pitch, LANES), F32),
                        pltpu.VMEM((nl, nb * pitch, LANES), F32),
                        pltpu.VMEM((nl, nb * pitch, LANES), F32),
                        pltpu.VMEM((2 * ns, nb * pitch, LANES), F32),
                        pltpu.VMEM((2 * ns, nb * pitch, LANES), F32),
                        pltpu.VMEM((nl, nb, LANES), F32),
                        pltpu.VMEM((2 * ns, nb, LANES), F32)],
        compiler_params=_cparams(("arbitrary",)),
        name="rec_prompt",
    )(u, *params)


def _rec_sample_kernel(u_ref, h0_ref, buf0_ref, s0r_ref, s0i_ref,
                       cw_ref, cb_ref, wg_ref, bg_ref, c8_ref, lbr_ref, lbi_ref,
                       bbd_ref, cbd_ref, d_ref, wglu_ref, bglu_ref,
                       o_ref, hl_ref, cbuf_ref, sre_ref, sim_ref, *, nt):
    ns = S5_P // LANES
    ext = [buf0_ref[j] for j in range(CONV_W - 1)] + [u_ref[t, :, 0:D_LRU] for t in range(nt)]
    xcs = []
    for t in range(nt):
        xc = cb_ref[...]
        for j in range(CONV_W):
            xc = xc + ext[t + j] * cw_ref[j:j + 1, :]
        xcs.append(xc)
    a, bb = _lru_gates(jnp.concatenate(xcs, axis=0), wg_ref, bg_ref, c8_ref)
    nbat = h0_ref.shape[0]
    h = h0_ref[...]
    for t in range(nt):
        h = a[t * nbat:(t + 1) * nbat] * h + bb[t * nbat:(t + 1) * nbat]
        o_ref[t, :, 0:D_LRU] = h.astype(o_ref.dtype)
    hl_ref[...] = h
    for j in range(CONV_W - 1):
        cbuf_ref[j] = ext[nt + j]

    sx = jnp.concatenate([u_ref[t, :, D_LRU:U_COLS] for t in range(nt)], axis=0)
    bu = jnp.dot(sx.astype(BF16), bbd_ref[...], preferred_element_type=F32)
    lr = jnp.concatenate([lbr_ref[j] for j in range(ns)], axis=1)
    li = jnp.concatenate([lbi_ref[j] for j in range(ns)], axis=1)
    xr = s0r_ref[...]
    xi = s0i_ref[...]
    xs = []
    for t in range(nt):
        but = bu[t * nbat:(t + 1) * nbat]
        bur = jnp.concatenate([but[:, 2 * j * LANES:(2 * j + 1) * LANES] for j in range(ns)], axis=1)
        bui = jnp.concatenate([but[:, (2 * j + 1) * LANES:(2 * j + 2) * LANES] for j in range(ns)], axis=1)
        xr, xi = lr * xr - li * xi + bur, lr * xi + li * xr + bui
        xs.append(jnp.concatenate([xr, xi], axis=1).astype(BF16))
    sre_ref[...] = xr
    sim_ref[...] = xi
    s_out = _s5_out(jnp.concatenate(xs, axis=0), sx, cbd_ref, d_ref, wglu_ref, bglu_ref)
    for t in range(nt):
        o_ref[t, :, D_LRU:U_COLS] = s_out[t * nbat:(t + 1) * nbat].astype(o_ref.dtype)


def _rec_sample(u_t, h0, buf0_t, s0r, s0i, pr):
    nt, nbat, _ = u_t.shape
    params = [pr["cw"], pr["cb"], pr["wg"], pr["bg"], pr["c8"], pr["lbr"], pr["lbi"],
              pr["bbd"], pr["cbd"], pr["d"], pr["wglu"], pr["bglu"]]
    return pl.pallas_call(
        functools.partial(_rec_sample_kernel, nt=nt),
        out_shape=[jax.ShapeDtypeStruct((nt, nbat, U_COLS), BF16),
                   jax.ShapeDtypeStruct((nbat, D_LRU), F32),
                   jax.ShapeDtypeStruct((CONV_W - 1, nbat, D_LRU), F32),
                   jax.ShapeDtypeStruct((nbat, S5_P), F32),
                   jax.ShapeDtypeStruct((nbat, S5_P), F32)],
        compiler_params=pltpu.CompilerParams(vmem_limit_bytes=VMEM_LIMIT),
        name="rec_sample",
    )(u_t, h0, buf0_t, s0r, s0i, *params)


def _rec_params(lru_conv_w, lru_conv_b, lru_w_r, lru_b_r, lru_w_i, lru_b_i, lru_lam,
                s5_a_re, s5_a_im, s5_log_dt, s5_b_re, s5_b_im, s5_c_re, s5_c_im, s5_d, s5_w_glu, s5_b_glu):
    ns = S5_P // LANES
    bd = lambda w: jax.scipy.linalg.block_diag(*[w[n] for n in range(w.shape[0])])
    wg = jnp.concatenate([bd(lru_w_r), bd(lru_w_i)], axis=1).astype(BF16)
    bg = jnp.concatenate([lru_b_r, lru_b_i]).reshape(1, 2 * D_LRU)
    c8 = (LRU_C * jax.nn.log_sigmoid(lru_lam.astype(F32))).reshape(1, D_LRU)
    dt = jnp.exp(s5_log_dt.astype(F32))[:, None]
    ar = s5_a_re.astype(F32)
    ai = s5_a_im.astype(F32)
    mag = jnp.exp(ar * dt)
    lb_re = mag * jnp.cos(ai * dt)
    lb_im = mag * jnp.sin(ai * dt)
    den = ar * ar + ai * ai
    n_re = lb_re - 1.0
    co_re = (n_re * ar + lb_im * ai) / den
    co_im = (lb_im * ar - n_re * ai) / den
    bb_re = co_re[..., None] * s5_b_re - co_im[..., None] * s5_b_im
    bb_im = co_re[..., None] * s5_b_im + co_im[..., None] * s5_b_re
    bre = bd(jnp.swapaxes(bb_re, 1, 2))
    bim = bd(jnp.swapaxes(bb_im, 1, 2))
    bbd = jnp.stack([bre.reshape(D_S5, ns, LANES), bim.reshape(D_S5, ns, LANES)], axis=2)
    bbd = bbd.reshape(D_S5, 2 * S5_P).astype(BF16)
    cre = bd(jnp.swapaxes(s5_c_re, 1, 2))
    cim = bd(jnp.swapaxes(s5_c_im, 1, 2))
    cbd = jnp.concatenate([cre, -cim], axis=0).astype(BF16)
    return dict(cw=lru_conv_w, cb=lru_conv_b.reshape(1, D_LRU), wg=wg, bg=bg, c8=c8,
                lbr=lb_re.reshape(ns, 1, LANES), lbi=lb_im.reshape(ns, 1, LANES),
                bbd=bbd, cbd=cbd, d=s5_d.reshape(1, D_S5),
                wglu=s5_w_glu.astype(BF16), bglu=s5_b_glu.reshape(1, D_S5))


def _moba_prompt_kernel(qt_ref, krm_ref, kt_ref, vt_ref, o_ref, km_s, vb_s, bias_s, sc_s, *, s, nblk, grp, nh):
    qi = pl.program_id(2)
    tq = MOBA_BLOCK
    hd = HEAD_DIM
    gk = grp * tq

    @pl.when(qi == 0)
    def _():
        r = lax.broadcasted_iota(jnp.int32, (nblk, s), 1) // MOBA_BLOCK
        n = lax.broadcasted_iota(jnp.int32, (nblk, s), 0)
        pm = jnp.where(r == n, 1.0 / MOBA_BLOCK, 0.0).astype(F32)
        for hh in range(nh):
            km_s[hh] = lax.dot_general(pm, kt_ref[0, hh * hd:(hh + 1) * hd, :], NT_DIMS,
                                       precision=HIGHEST, preferred_element_type=F32)
        ones_row = (lax.broadcasted_iota(jnp.int32, (VB_ROWS - hd, gk), 0) == 0).astype(BF16)
        for g in range(nblk // grp):
            for hh in range(nh):
                vb_s[g, hh, 0:hd, :] = vt_ref[0, hh * hd:(hh + 1) * hd, g * gk:(g + 1) * gk].astype(BF16)
                vb_s[g, hh, hd:VB_ROWS, :] = ones_row

    qt = qt_ref[0]
    qs = (qt * (hd ** -0.5 * LOG2E)).astype(BF16)
    feat = lax.broadcasted_iota(jnp.int32, (nh * hd, tq), 0)
    blk_iota = lax.broadcasted_iota(jnp.int32, (nblk, tq), 0)
    past = blk_iota < qi
    ws = []
    for hh in range(nh):
        ws.append(jnp.where((feat >= hh * hd) & (feat < (hh + 1) * hd), qs, jnp.zeros_like(qs)))
        sb = jnp.dot(km_s[hh], qt[hh * hd:(hh + 1) * hd, :], precision=HIGHEST,
                     preferred_element_type=F32)
        sb = jnp.where(past, sb, -jnp.inf)
        cnt = jnp.zeros((nblk, tq), jnp.int32)
        for i in range(nblk):
            ri = sb[i:i + 1, :]
            beats = (ri > sb) | ((ri == sb) & (i < blk_iota))
            cnt = cnt + beats.astype(jnp.int32)
        sel = (cnt < MOBA_TOPK) & past
        bias_s[hh] = jnp.where(sel | (blk_iota == qi), 0.0, NEG).astype(F32)

    def scores(g, hh):
        st = pl.multiple_of(g * gk, gk)
        sc = jnp.dot(krm_ref[0, pl.ds(st, gk), :], ws[hh], preferred_element_type=F32)
        return jnp.concatenate(
            [sc[i * tq:(i + 1) * tq] + bias_s[hh, pl.ds(g * grp + i, 1), :] for i in range(grp)], axis=0)

    def pass1(g, ms):
        st = pl.multiple_of(g * gk, gk)
        out = []
        for hh in range(nh):
            sc = scores(g, hh)
            sc_s[hh, pl.ds(st, gk), :] = sc
            out.append(jnp.maximum(ms[hh], jnp.max(sc, axis=0, keepdims=True)))
        return tuple(out)

    g_own = qi // grp
    ms = lax.fori_loop(0, g_own, pass1, tuple(jnp.full((1, tq), NEG, F32) for _ in range(nh)))
    key = lax.broadcasted_iota(jnp.int32, (gk, tq), 0)
    qry = lax.broadcasted_iota(jnp.int32, (gk, tq), 1)
    causal = key <= qry + (qi * tq - g_own * gk)
    st_own = pl.multiple_of(g_own * gk, gk)
    ms = list(ms)
    for hh in range(nh):
        sc = jnp.where(causal, scores(g_own, hh), NEG)
        sc_s[hh, pl.ds(st_own, gk), :] = sc
        ms[hh] = jnp.maximum(ms[hh], jnp.max(sc, axis=0, keepdims=True))

    def pass2(g, accs):
        st = pl.multiple_of(g * gk, gk)
        out = []
        for hh in range(nh):
            p = jnp.exp2(sc_s[hh, pl.ds(st, gk), :] - ms[hh])
            out.append(accs[hh] + jnp.dot(vb_s[g, hh], p.astype(BF16), preferred_element_type=F32))
        return tuple(out)

    accs = lax.fori_loop(0, g_own + 1, pass2, tuple(jnp.zeros((VB_ROWS, tq), F32) for _ in range(nh)))
    out_t = jnp.concatenate([acc[0:hd] / acc[hd:hd + 1] for acc in accs], axis=0)
    o_ref[0] = jnp.transpose(out_t).astype(o_ref.dtype)


def _moba_prompt(qt, krm, kt, vt):
    nb, _, s = qt.shape
    nblk = s // MOBA_BLOCK
    tq = MOBA_BLOCK
    nh = 4
    grp = 4
    hp = nh * HEAD_DIM
    once = dict(pipeline_mode=pl.Buffered(1))
    return pl.pallas_call(
        functools.partial(_moba_prompt_kernel, s=s, nblk=nblk, grp=grp, nh=nh),
        grid=(nb, N_HEADS // nh, nblk),
        in_specs=[pl.BlockSpec((1, hp, tq), lambda b, h, i: (b, h, i)),
                  pl.BlockSpec((1, s, hp), lambda b, h, i: (b, 0, h), **once),
                  pl.BlockSpec((1, hp, s), lambda b, h, i: (b, h, 0), **once),
                  pl.BlockSpec((1, hp, s), lambda b, h, i: (b, h, 0), **once)],
        out_specs=pl.BlockSpec((1, tq, hp), lambda b, h, i: (b, i, h)),
        out_shape=jax.ShapeDtypeStruct((nb, s, D_ATT), BF16),
        scratch_shapes=[pltpu.VMEM((nh, nblk, HEAD_DIM), F32),
                        pltpu.VMEM((nblk // grp, nh, VB_ROWS, grp * tq), BF16),
                        pltpu.VMEM((nh, nblk, tq), F32),
                        pltpu.VMEM((nh, s, tq), F32)],
        compiler_params=_cparams(("arbitrary", "arbitrary", "arbitrary")),
        name="moba_prompt",
    )(qt, krm, kt, vt)


PAGES_PER_BLOCK = MOBA_BLOCK // PAGE_SIZE
N_PAGES = PAST_LEN // PAGE_SIZE
N_FULL = PAST_LEN // MOBA_BLOCK
PG = 16


def _page_copy(cache_ref, pt_ref, buf, sem, layer, step, slot, j):
    pid = pt_ref[step * PG + j]
    return pltpu.make_async_copy(cache_ref.at[layer, pid], buf.at[slot, j], sem.at[slot, j])


def _ksum_kernel(pt_ref, cache_ref, o_ref, buf, sem, *, layer, nsteps, steps_per_b):
    n = pl.program_id(0)
    slot = n % 2
    g = n % steps_per_b
    bpg = PG // PAGES_PER_BLOCK

    def start(step, sl):
        for j in range(PG):
            _page_copy(cache_ref, pt_ref, buf, sem, layer, step, sl, j).start()

    @pl.when(n == 0)
    def _():
        start(0, 0)

    @pl.when(n + 1 < nsteps)
    def _():
        start(n + 1, 1 - slot)

    @pl.when(g == 0)
    def _():
        o_ref[...] = jnp.zeros(o_ref.shape, F32)

    for j in range(PG):
        pltpu.make_async_copy(buf.at[slot, j], buf.at[slot, j], sem.at[slot, j]).wait()
    lane = lax.broadcasted_iota(jnp.int32, (HEAD_DIM, LANES), 1)
    for hh in range(N_HEADS):
        acc = o_ref[0, hh]
        for blk in range(bpg):
            tile = buf[slot, PAGES_PER_BLOCK * blk, hh]
            for pp in range(1, PAGES_PER_BLOCK):
                tile = tile + buf[slot, PAGES_PER_BLOCK * blk + pp, hh]
            acc = jnp.where(lane == g * bpg + blk, jnp.sum(tile, axis=1, keepdims=True), acc)
        o_ref[0, hh] = acc


def _block_ksum(cache_kt, pt_flat, layer, nbat):
    steps_per_b = N_PAGES // PG
    nsteps = nbat * steps_per_b
    return pl.pallas_call(
        functools.partial(_ksum_kernel, layer=layer, nsteps=nsteps, steps_per_b=steps_per_b),
        grid_spec=pltpu.PrefetchScalarGridSpec(
            num_scalar_prefetch=1,
            grid=(nsteps,),
            in_specs=[pl.BlockSpec(memory_space=pl.ANY)],
            out_specs=pl.BlockSpec((1, N_HEADS, HEAD_DIM, LANES),
                                   lambda n, pt: (n // steps_per_b, 0, 0, 0)),
            scratch_shapes=[pltpu.VMEM((2, PG, N_HEADS, HEAD_DIM, PAGE_SIZE), F32),
                            pltpu.SemaphoreType.DMA((2, PG))]),
        out_shape=jax.ShapeDtypeStruct((nbat, N_HEADS, HEAD_DIM, LANES), F32),
        compiler_params=_cparams(("arbitrary",)),
        name="block_ksum",
    )(pt_flat, cache_kt)


def _topk_kernel(q_ref, ks_ref, o_ref):
    nq = q_ref.shape[2]
    lane = lax.broadcasted_iota(jnp.int32, (nq, LANES), 1)
    for hh in range(N_HEADS):
        km = ks_ref[0, hh] * (1.0 / MOBA_BLOCK)
        sb = jnp.dot(q_ref[0, hh], km, precision=HIGHEST, preferred_element_type=F32)
        sb = jnp.where(lane < N_FULL, sb, -jnp.inf)
        out = jnp.zeros((nq, LANES), jnp.int32)
        for r in range(MOBA_TOPK):
            mx = jnp.max(sb, axis=1, keepdims=True)
            idx = jnp.min(jnp.where(sb == mx, lane, LANES), axis=1, keepdims=True)
            out = jnp.where(lane == r, idx, out)
            sb = jnp.where(lane == idx, -jnp.inf, sb)
        o_ref[0, hh] = out


def _sample_topk(q_pad, ksum_t):
    nbat, nh, nq, hd = q_pad.shape
    return pl.pallas_call(
        _topk_kernel,
        grid=(nbat,),
        in_specs=[pl.BlockSpec((1, nh, nq, hd), lambda b: (b, 0, 0, 0)),
                  pl.BlockSpec((1, nh, hd, LANES), lambda b: (b, 0, 0, 0))],
        out_specs=pl.BlockSpec((1, nh, nq, LANES), lambda b: (b, 0, 0, 0)),
        out_shape=jax.ShapeDtypeStruct((nbat, nh, nq, LANES), jnp.int32),
        compiler_params=_cparams(("arbitrary",)),
        name="sample_topk",
    )(q_pad, ksum_t)


def _slab_copy(src, buf, sem, slot, j):
    return pltpu.make_async_copy(src, buf.at[slot, j], sem.at[slot, j])


def _moba_sample_kernel(idx_ref, pt_ref, q_ref, kn_ref, vn_ref, ck_ref, cv_ref, o_ref,
                        kbuf, vbuf, ksem, vsem, *, layer, nsteps, nt):
    n = pl.program_id(0)
    slot = n % 2
    b = n // N_HEADS
    nsel = MOBA_TOPK * PAGES_PER_BLOCK

    def start(step, sl):
        sb = step // N_HEADS
        sh = step % N_HEADS
        for t in range(nt):
            for r in range(MOBA_TOPK):
                blk = idx_ref[(step * nt + t) * MOBA_TOPK + r]
                for pp in range(PAGES_PER_BLOCK):
                    pid = pt_ref[sb * N_PAGES + blk * PAGES_PER_BLOCK + pp]
                    j = (t * MOBA_TOPK + r) * PAGES_PER_BLOCK + pp
                    _slab_copy(ck_ref.at[layer, pid, sh], kbuf, ksem, sl, j).start()
                    _slab_copy(cv_ref.at[layer, pid, sh], vbuf, vsem, sl, j).start()

    def wait(sl):
        for j in range(nt * nsel):
            _slab_copy(kbuf.at[sl, j], kbuf, ksem, sl, j).wait()
            _slab_copy(vbuf.at[sl, j], vbuf, vsem, sl, j).wait()

    @pl.when(n == 0)
    def _():
        start(0, 0)

    wait(slot)
    start(jnp.minimum(n + 1, nsteps - 1), 1 - slot)

    qs = (q_ref[0, 0] * (HEAD_DIM ** -0.5)).astype(BF16)
    nq = qs.shape[0]
    nkeys = nsel * PAGE_SIZE
    qrow = lax.broadcasted_iota(jnp.int32, (nq, nkeys), 0)
    scores = []
    for t in range(nt):
        kt = jnp.concatenate([kbuf[slot, t * nsel + i] for i in range(nsel)], axis=1).astype(BF16)
        st = jnp.dot(qs, kt, preferred_element_type=F32)
        scores.append(jnp.where(qrow == t, st, -jnp.inf))
    ntok = kn_ref.shape[2]
    so = jnp.dot(qs, kn_ref[0].astype(BF16), preferred_element_type=F32)
    orow = lax.broadcasted_iota(jnp.int32, (nq, ntok), 0)
    opos = lax.broadcasted_iota(jnp.int32, (nq, ntok), 1) - b * nt
    so = jnp.where((opos >= 0) & (opos <= orow), so, -jnp.inf)
    m = jnp.max(so, axis=1, keepdims=True)
    for st in scores:
        m = jnp.maximum(m, jnp.max(st, axis=1, keepdims=True))
    po = jnp.exp(so - m)
    den = jnp.sum(po, axis=1, keepdims=True)
    acc = lax.dot_general(po.astype(BF16), vn_ref[0].astype(BF16), NT_DIMS, preferred_element_type=F32)
    for t in range(nt):
        pt_ = jnp.exp(scores[t] - m)
        den = den + jnp.sum(pt_, axis=1, keepdims=True)
        vt = jnp.concatenate([vbuf[slot, t * nsel + i] for i in range(nsel)], axis=1).astype(BF16)
        acc = acc + lax.dot_general(pt_.astype(BF16), vt, NT_DIMS, preferred_element_type=F32)
    o_ref[0, 0] = acc / den

    @pl.when(n == nsteps - 1)
    def _():
        wait(1 - slot)


def _moba_sample(q_pad, kt_new, vt_new, cache_kt, cache_vt, idx_flat, pt_flat, layer, nt):
    nbat, nh, nq, hd = q_pad.shape
    ntok = kt_new.shape[1]
    nsteps = nbat * nh
    nslab = nt * MOBA_TOPK * PAGES_PER_BLOCK
    qtok = pl.BlockSpec((1, 1, nq, hd), lambda n, idx, pt: (n // nh, n % nh, 0, 0))
    new = pl.BlockSpec((1, hd, ntok), lambda n, idx, pt: (n % nh, 0, 0))
    return pl.pallas_call(
        functools.partial(_moba_sample_kernel, layer=layer, nsteps=nsteps, nt=nt),
        grid_spec=pltpu.PrefetchScalarGridSpec(
            num_scalar_prefetch=2,
            grid=(nsteps,),
            in_specs=[qtok, new, new, pl.BlockSpec(memory_space=pl.ANY), pl.BlockSpec(memory_space=pl.ANY)],
            out_specs=qtok,
            scratch_shapes=[pltpu.VMEM((2, nslab, hd, PAGE_SIZE), F32),
                            pltpu.VMEM((2, nslab, hd, PAGE_SIZE), F32),
                            pltpu.SemaphoreType.DMA((2, nslab)),
                            pltpu.SemaphoreType.DMA((2, nslab))]),
        out_shape=jax.ShapeDtypeStruct((nbat, nh, nq, hd), F32),
        compiler_params=_cparams(("arbitrary",)),
        name="moba_sample",
    )(idx_flat, pt_flat, q_pad, kt_new.reshape(nh, hd, ntok), vt_new.reshape(nh, hd, ntok),
      cache_kt, cache_vt)


def _rope_tables(pos):
    inv = jnp.power(ROPE_THETA, -jnp.arange(HALF, dtype=F32) / HALF)
    ang = pos.astype(F32)[:, None] * inv[None, :]
    cos, sin = jnp.cos(ang), jnp.sin(ang)
    reps = LANES // HEAD_DIM
    return (jnp.tile(jnp.concatenate([cos, cos], axis=1), (1, reps)),
            jnp.tile(jnp.concatenate([-sin, sin], axis=1), (1, reps)),
            cos.T, sin.T)


def _split_w_in(w):
    a_x, a_g, q, k, v, b_g, s_x, s_g = jnp.split(
        w, (D_LRU, 2 * D_LRU, 2 * D_LRU + D_ATT, 2 * D_LRU + 2 * D_ATT, 2 * D_LRU + 3 * D_ATT,
            2 * D_LRU + 4 * D_ATT, 2 * D_LRU + 4 * D_ATT + D_S5), axis=1)
    w_row = jnp.concatenate([a_x, s_x, a_g, b_g, s_g, k], axis=1).astype(BF16)
    w_t = jnp.concatenate([q, k, v], axis=1).T.astype(BF16)
    return w_row, w_t


def kernel(x_prompt, x_sample, cache_k, cache_v, state_lru_h, state_lru_conv, state_s5_re, state_s5_im,
           page_table, c_prompt, c_sample, norm_g, w_ada, b_ada, w_in, w_out, lru_conv_w, lru_conv_b,
           lru_w_r, lru_b_r, lru_w_i, lru_b_i, lru_lam, s5_a_re, s5_a_im, s5_log_dt, s5_b_re, s5_b_im,
           s5_c_re, s5_c_im, s5_d, s5_w_glu, s5_b_glu, final_g):
    nbp, seq, _ = x_prompt.shape
    nbs, nts, _ = x_sample.shape
    ntok_s = nbs * nts

    rope_p = _rope_tables(jnp.arange(seq, dtype=jnp.int32))
    rope_s = _rope_tables(PAST_LEN + (jnp.arange(ntok_s, dtype=jnp.int32) % nts))

    n_c = nbp + nbs
    c_all = jnp.concatenate([c_prompt, c_sample, jnp.zeros((-n_c % 8, D_MODEL), F32)], axis=0)
    mod = _modulation(c_all, w_ada, b_ada)

    cache_kt = jnp.swapaxes(cache_k, 3, 4)
    cache_vt = jnp.swapaxes(cache_v, 3, 4)
    pt_flat = page_table.reshape(-1)
    xp = x_prompt
    xs = x_sample.reshape(1, ntok_s, D_MODEL)
    outs_p = [[] for _ in range(6)]
    outs_s = [[] for _ in range(6)]
    for l in range(DEPTH):
        last = l == DEPTH - 1
        w_row, w_t = _split_w_in(w_in[l])
        w_o = w_out[l].astype(BF16)
        pr = _rec_params(lru_conv_w[l], lru_conv_b[l], lru_w_r[l], lru_b_r[l], lru_w_i[l], lru_b_i[l],
                         lru_lam[l], s5_a_re[l], s5_a_im[l], s5_log_dt[l], s5_b_re[l], s5_b_im[l],
                         s5_c_re[l], s5_c_im[l], s5_d[l], s5_w_glu[l], s5_b_glu[l])
        shift, scale, gate = jnp.split(mod[l], 3, axis=-1)

        mp = lambda a: a[:nbp].reshape(nbp, 1, D_MODEL)
        u, gates, krm, qt, kt, vt = _inproj(xp, mp(scale), mp(shift), norm_g[l], w_row, w_t, rope_p, tm=512)
        as_out, h_last, cbuf, s_re, s_im = _rec_prompt(u, pr, tc=256)
        att = _moba_prompt(qt, krm, kt, vt)
        xp = _outproj(xp, as_out, att, gates, mp(gate), w_o, final_g, tm=512, final=last)
        for lst, val in zip(outs_p, (kt.reshape(nbp, N_HEADS, HEAD_DIM, seq), vt.reshape(nbp, N_HEADS, HEAD_DIM, seq),
                                     h_last, cbuf, s_re.reshape(nbp, S5_GROUPS, S5_STATE),
                                     s_im.reshape(nbp, S5_GROUPS, S5_STATE))):
            lst.append(val)

        ms = lambda a: jnp.repeat(a[nbp:n_c], nts, axis=0).reshape(1, ntok_s, D_MODEL)
        u, gates, _, qt, kt, vt = _inproj(xs, ms(scale), ms(shift), norm_g[l], w_row, w_t, rope_s, tm=ntok_s)
        to_heads = lambda a: a.reshape(N_HEADS, HEAD_DIM, nbs, nts).transpose(2, 0, 3, 1)
        q, k, v = to_heads(qt), to_heads(kt), to_heads(vt)
        u_t = u.reshape(nbs, nts, U_COLS).transpose(1, 0, 2)
        as_t, h_last, cbuf_t, s_re, s_im = _rec_sample(
            u_t, state_lru_h[l], state_lru_conv[l].transpose(1, 0, 2),
            state_s5_re[l].reshape(nbs, S5_P), state_s5_im[l].reshape(nbs, S5_P), pr)
        as_out = as_t.transpose(1, 0, 2).reshape(1, ntok_s, U_COLS)
        ksum_t = _block_ksum(cache_kt, pt_flat, l, nbs)
        q_pad = jnp.pad(q, ((0, 0), (0, 0), (0, -nts % 8), (0, 0)))
        idx = _sample_topk(q_pad, ksum_t)[:, :, :nts, :MOBA_TOPK].reshape(-1)
        att = _moba_sample(q_pad, kt[0], vt[0], cache_kt, cache_vt, idx, pt_flat, l, nts)[:, :, :nts]
        att = att.transpose(0, 2, 1, 3).reshape(1, ntok_s, D_ATT).astype(BF16)
        xs = _outproj(xs, as_out, att, gates, ms(gate), w_o, final_g, tm=ntok_s, final=last)
        for lst, val in zip(outs_s, (k, v, h_last, cbuf_t.transpose(1, 0, 2),
                                     s_re.reshape(nbs, S5_GROUPS, S5_STATE),
                                     s_im.reshape(nbs, S5_GROUPS, S5_STATE))):
            lst.append(val)

    k_p, v_p = (jnp.swapaxes(jnp.stack(o), 3, 4) for o in outs_p[:2])
    return (xp, xs.reshape(nbs, nts, D_MODEL), k_p, v_p,
            *[jnp.stack(o) for o in outs_p[2:]], *[jnp.stack(o) for o in outs_s])
```

```python
import functools

import jax
import jax.numpy as jnp
from jax import lax
from jax.experimental import pallas as pl
from jax.experimental.pallas import tpu as pltpu

F32 = jnp.float32
BF16 = jnp.bfloat16
HIGHEST = lax.Precision.HIGHEST

D_MODEL = 1024
DEPTH = 2
PAST_LEN = 16384
PAGE_SIZE = 128
D_LRU = 256
LRU_BLOCKS = 4
LRU_BW = D_LRU // LRU_BLOCKS
CONV_W = 4
LRU_C = 8.0
N_HEADS = 8
HEAD_DIM = 64
HALF = HEAD_DIM // 2
D_ATT = N_HEADS * HEAD_DIM
MOBA_BLOCK = 256
MOBA_TOPK = 3
ROPE_THETA = 10000.0
D_S5 = 256
S5_GROUP = 16
S5_GROUPS = D_S5 // S5_GROUP
S5_STATE = 64
S5_P = S5_GROUPS * S5_STATE
D_MIX = D_LRU + D_ATT + D_S5
EPS = 1e-6

LANES = 128
BF16_ROWS = 16
NEG = -1e30
LOG2E = 1.4426950408889634
VB_ROWS = HEAD_DIM + BF16_ROWS

U_COLS = D_LRU + D_S5
G_COLS = D_MIX
OFF_U = 0
OFF_G = OFF_U + U_COLS
OFF_KR = OFF_G + G_COLS

VMEM_LIMIT = 56 * 1024 * 1024

NT_DIMS = (((1,), (1,)), ((), ()))


def _cparams(sem):
    return pltpu.CompilerParams(dimension_semantics=sem, vmem_limit_bytes=VMEM_LIMIT)


def _silu(x):
    return x * jax.nn.sigmoid(x)


def _mod_kernel(c_ref, w_ref, b_ref, o_ref):
    o_ref[0] = jnp.dot(_silu(c_ref[...]), w_ref[0], precision=HIGHEST,
                       preferred_element_type=F32) + b_ref[0]


def _modulation(c_all, w_ada, b_ada):
    n = c_all.shape[0]
    tn = D_MODEL
    return pl.pallas_call(
        _mod_kernel,
        grid=(DEPTH, 3 * D_MODEL // tn),
        in_specs=[pl.BlockSpec((n, D_MODEL), lambda l, j: (0, 0)),
                  pl.BlockSpec((1, D_MODEL, tn), lambda l, j: (l, 0, j)),
                  pl.BlockSpec((1, 1, tn), lambda l, j: (l, 0, j))],
        out_specs=pl.BlockSpec((1, n, tn), lambda l, j: (l, 0, j)),
        out_shape=jax.ShapeDtypeStruct((DEPTH, n, 3 * D_MODEL), F32),
        compiler_params=_cparams(("arbitrary", "arbitrary")),
        name="modulation",
    )(c_all, w_ada, b_ada.reshape(DEPTH, 1, 3 * D_MODEL))


def _inproj_kernel(x_ref, sc_ref, sh_ref, g_ref, w_ref, wt_ref, cos_ref, sin_ref, cost_ref, sint_ref,
                   u_ref, gate_ref, krm_ref, qt_ref, kt_ref, vt_ref):
    x = x_ref[0]
    h = x * lax.rsqrt(jnp.mean(x * x, axis=-1, keepdims=True) + EPS) * g_ref[...]
    h = h * (1.0 + sc_ref[0]) + sh_ref[0]
    hb = h.astype(BF16)

    def proj(lo, n):
        return jnp.dot(hb, w_ref[:, lo:lo + n], preferred_element_type=F32)

    u_ref[0] = proj(OFF_U, U_COLS)
    gate_ref[0] = _silu(proj(OFF_G, G_COLS)).astype(gate_ref.dtype)

    cos = cos_ref[...]
    sin = sin_ref[...]
    lane = lax.broadcasted_iota(jnp.int32, cos.shape, 1)
    first_half = (lane & (HEAD_DIM - 1)) < HALF
    p = proj(OFF_KR, D_ATT)
    for j in range(D_ATT // LANES):
        xs = p[:, j * LANES:(j + 1) * LANES]
        partner = jnp.where(first_half, pltpu.roll(xs, LANES - HALF, 1), pltpu.roll(xs, HALF, 1))
        krm_ref[0, :, j * LANES:(j + 1) * LANES] = (xs * cos + partner * sin).astype(BF16)

    cost = cost_ref[...]
    sint = sint_ref[...]
    for n, (ref, use_rope) in enumerate(((qt_ref, True), (kt_ref, True), (vt_ref, False))):
        pt = lax.dot_general(wt_ref[n * D_ATT:(n + 1) * D_ATT, :], hb, NT_DIMS,
                             preferred_element_type=F32)
        if not use_rope:
            ref[0] = pt
            continue
        for hh in range(N_HEADS):
            r0 = hh * HEAD_DIM
            x1 = pt[r0:r0 + HALF]
            x2 = pt[r0 + HALF:r0 + HEAD_DIM]
            ref[0, r0:r0 + HALF, :] = x1 * cost - x2 * sint
            ref[0, r0 + HALF:r0 + HEAD_DIM, :] = x2 * cost + x1 * sint


def _inproj(x, scale, shift, g, w_row, w_t, rope, tm):
    nb, s, _ = x.shape
    cos_r, sin_r, cos_t, sin_t = rope
    rm = scale.shape[1]
    mod_block = (1, tm, D_MODEL) if rm == s else (1, 1, D_MODEL)
    mod_map = (lambda b, i: (b, i, 0)) if rm == s else (lambda b, i: (b, 0, 0))
    fm_spec = pl.BlockSpec((1, D_ATT, tm), lambda b, i: (b, 0, i))
    fm_shape = jax.ShapeDtypeStruct((nb, D_ATT, s), F32)
    return pl.pallas_call(
        _inproj_kernel,
        grid=(nb, s // tm),
        in_specs=[pl.BlockSpec((1, tm, D_MODEL), lambda b, i: (b, i, 0)),
                  pl.BlockSpec(mod_block, mod_map),
                  pl.BlockSpec(mod_block, mod_map),
                  pl.BlockSpec((1, D_MODEL), lambda b, i: (0, 0)),
                  pl.BlockSpec(w_row.shape, lambda b, i: (0, 0)),
                  pl.BlockSpec(w_t.shape, lambda b, i: (0, 0)),
                  pl.BlockSpec((tm, LANES), lambda b, i: (i, 0)),
                  pl.BlockSpec((tm, LANES), lambda b, i: (i, 0)),
                  pl.BlockSpec((HALF, tm), lambda b, i: (0, i)),
                  pl.BlockSpec((HALF, tm), lambda b, i: (0, i))],
        out_specs=[pl.BlockSpec((1, tm, U_COLS), lambda b, i: (b, i, 0)),
                   pl.BlockSpec((1, tm, G_COLS), lambda b, i: (b, i, 0)),
                   pl.BlockSpec((1, tm, D_ATT), lambda b, i: (b, i, 0)),
                   fm_spec, fm_spec, fm_spec],
        out_shape=[jax.ShapeDtypeStruct((nb, s, U_COLS), F32),
                   jax.ShapeDtypeStruct((nb, s, G_COLS), BF16),
                   jax.ShapeDtypeStruct((nb, s, D_ATT), BF16),
                   fm_shape, fm_shape, fm_shape],
        compiler_params=_cparams(("arbitrary", "arbitrary")),
        name="inproj",
    )(x, scale, shift, g.reshape(1, D_MODEL), w_row, w_t, cos_r, sin_r, cos_t, sin_t)


def _outproj_kernel(x_ref, as_ref, att_ref, gate_ref, gm_ref, w_ref, fg_ref, o_ref, *, final):
    g = gate_ref[0].astype(F32)
    a_s = as_ref[0].astype(F32)
    mixed = jnp.concatenate([a_s[:, :D_LRU] * g[:, :D_LRU],
                             att_ref[0].astype(F32) * g[:, D_LRU:D_LRU + D_ATT],
                             a_s[:, D_LRU:] * g[:, D_LRU + D_ATT:]], axis=1).astype(BF16)
    y = jnp.dot(mixed, w_ref[...], preferred_element_type=F32)
    xn = x_ref[0] + gm_ref[0] * y
    if final:
        xn = xn * lax.rsqrt(jnp.mean(xn * xn, axis=-1, keepdims=True) + EPS) * fg_ref[...]
    o_ref[0] = xn


def _outproj(x, as_out, att, gates, gate_mod, w_out, final_g, tm, final):
    nb, s, _ = x.shape
    rm = gate_mod.shape[1]
    mod_block = (1, tm, D_MODEL) if rm == s else (1, 1, D_MODEL)
    mod_map = (lambda b, i: (b, i, 0)) if rm == s else (lambda b, i: (b, 0, 0))
    row = lambda n: pl.BlockSpec((1, tm, n), lambda b, i: (b, i, 0))
    return pl.pallas_call(
        functools.partial(_outproj_kernel, final=final),
        grid=(nb, s // tm),
        in_specs=[row(D_MODEL), row(U_COLS), row(D_ATT), row(G_COLS),
                  pl.BlockSpec(mod_block, mod_map),
                  pl.BlockSpec((D_MIX, D_MODEL), lambda b, i: (0, 0)),
                  pl.BlockSpec((1, D_MODEL), lambda b, i: (0, 0))],
        out_specs=row(D_MODEL),
        out_shape=jax.ShapeDtypeStruct((nb, s, D_MODEL), F32),
        compiler_params=_cparams(("arbitrary", "arbitrary")),
        name="outproj",
    )(x, as_out, att, gates, gate_mod, w_out, final_g.reshape(1, D_MODEL))


def _lru_gates(xc, wg_ref, bg_ref, c8_ref):
    gates = jnp.dot(xc.astype(BF16), wg_ref[...], preferred_element_type=F32) + bg_ref[...]
    r = jax.nn.sigmoid(gates[:, :D_LRU])
    i = jax.nn.sigmoid(gates[:, D_LRU:])
    log_a = r * c8_ref[...]
    a = jnp.exp(log_a)
    mult = jnp.sqrt(1.0 - jnp.exp(2.0 * log_a))
    return a, mult * i * xc


def _s5_out(x_cat_bf16, sx, cbd_ref, d_ref, wglu_ref, bglu_ref):
    y = jnp.dot(x_cat_bf16, cbd_ref[...], preferred_element_type=F32) + d_ref[...] * sx
    z = jax.nn.gelu(y)
    return z * jax.nn.sigmoid(jnp.dot(z.astype(BF16), wglu_ref[...], preferred_element_type=F32)
                              + bglu_ref[...])


def _rec_prompt_kernel(u_ref, cw_ref, cb_ref, wg_ref, bg_ref, c8_ref, lbr_ref, lbi_ref,
                       bbd_ref, cbd_ref, d_ref, wglu_ref, bglu_ref,
                       o_ref, hl_ref, cbuf_ref, sre_ref, sim_ref,
                       ext, a_s, b_s, h_s, bu_s, x_s, h_st, x_st, *, nb, tc, pitch):
    c = pl.program_id(0)
    nl = D_LRU // LANES
    ns = S5_P // LANES

    @pl.when(c == 0)
    def _():
        ext[:, 0:8, :] = jnp.zeros((nb, 8, D_LRU), F32)
        h_st[...] = jnp.zeros(h_st.shape, F32)
        x_st[...] = jnp.zeros(x_st.shape, F32)

    for b in range(nb):
        r0 = b * pitch
        ext[b, 8:8 + tc, :] = u_ref[b, :, 0:D_LRU]
        xc = cb_ref[...]
        for j in range(CONV_W):
            xc = xc + ext[b, 8 - (CONV_W - 1) + j:8 - (CONV_W - 1) + j + tc, :] * cw_ref[j:j + 1, :]
        ext[b, 8 - (CONV_W - 1):8, :] = ext[b, 8 + tc - (CONV_W - 1):8 + tc, :]
        a, bb = _lru_gates(xc, wg_ref, bg_ref, c8_ref)
        for j in range(nl):
            a_s[j, r0:r0 + tc, :] = a[:, j * LANES:(j + 1) * LANES]
            b_s[j, r0:r0 + tc, :] = bb[:, j * LANES:(j + 1) * LANES]
        sxb = u_ref[b, :, D_LRU:U_COLS].astype(BF16)
        for j in range(ns):
            bu = jnp.dot(sxb, bbd_ref[:, 2 * j * LANES:(2 * j + 2) * LANES], preferred_element_type=F32)
            bu_s[j, r0:r0 + tc, :] = bu[:, :LANES]
            bu_s[ns + j, r0:r0 + tc, :] = bu[:, LANES:]

    def step(t, carry):
        h, xr, xi = carry
        rows = pl.ds(t, nb, stride=pitch)
        h_new, xr_new, xi_new = [], [], []
        for j in range(nl):
            hj = a_s[j, rows, :] * h[j] + b_s[j, rows, :]
            h_s[j, rows, :] = hj
            h_new.append(hj)
        for j in range(ns):
            lr = lbr_ref[j]
            li = lbi_ref[j]
            nr = lr * xr[j] - li * xi[j] + bu_s[j, rows, :]
            ni = lr * xi[j] + li * xr[j] + bu_s[ns + j, rows, :]
            x_s[j, rows, :] = nr
            x_s[ns + j, rows, :] = ni
            xr_new.append(nr)
            xi_new.append(ni)
        return tuple(h_new), tuple(xr_new), tuple(xi_new)

    init = (tuple(h_st[j] for j in range(nl)),
            tuple(x_st[j] for j in range(ns)),
            tuple(x_st[ns + j] for j in range(ns)))
    h, xr, xi = lax.fori_loop(0, tc, step, init)
    for j in range(nl):
        h_st[j] = h[j]
        hl_ref[:, j * LANES:(j + 1) * LANES] = h[j]
    for j in range(ns):
        x_st[j] = xr[j]
        x_st[ns + j] = xi[j]
        sre_ref[:, j * LANES:(j + 1) * LANES] = xr[j]
        sim_ref[:, j * LANES:(j + 1) * LANES] = xi[j]

    for b in range(nb):
        r0 = b * pitch
        for j in range(nl):
            o_ref[b, :, j * LANES:(j + 1) * LANES] = h_s[j, r0:r0 + tc, :].astype(o_ref.dtype)
        x_cat = jnp.concatenate([x_s[j, r0:r0 + tc, :].astype(BF16) for j in range(2 * ns)], axis=1)
        o_ref[b, :, D_LRU:U_COLS] = _s5_out(x_cat, u_ref[b, :, D_LRU:U_COLS], cbd_ref, d_ref,
                                            wglu_ref, bglu_ref).astype(o_ref.dtype)
        cbuf_ref[b] = ext[b, 8 - (CONV_W - 1):8, :]


def _rec_prompt(u, pr, tc):
    nb, s, _ = u.shape
    pitch = tc + 8
    nl = D_LRU // LANES
    ns = S5_P // LANES
    full = lambda a: pl.BlockSpec(a.shape, lambda c, _n=a.ndim: (0,) * _n)
    params = [pr["cw"], pr["cb"], pr["wg"], pr["bg"], pr["c8"], pr["lbr"], pr["lbi"],
              pr["bbd"], pr["cbd"], pr["d"], pr["wglu"], pr["bglu"]]
    st = lambda *shape: pl.BlockSpec(shape, lambda c, _n=len(shape): (0,) * _n)
    return pl.pallas_call(
        functools.partial(_rec_prompt_kernel, nb=nb, tc=tc, pitch=pitch),
        grid=(s // tc,),
        in_specs=[pl.BlockSpec((nb, tc, U_COLS), lambda c: (0, c, 0))] + [full(a) for a in params],
        out_specs=[pl.BlockSpec((nb, tc, U_COLS), lambda c: (0, c, 0)),
                   st(nb, D_LRU), st(nb, CONV_W - 1, D_LRU), st(nb, S5_P), st(nb, S5_P)],
        out_shape=[jax.ShapeDtypeStruct((nb, s, U_COLS), BF16),
                   jax.ShapeDtypeStruct((nb, D_LRU), F32),
                   jax.ShapeDtypeStruct((nb, CONV_W - 1, D_LRU), F32),
                   jax.ShapeDtypeStruct((nb, S5_P), F32),
                   jax.ShapeDtypeStruct((nb, S5_P), F32)],
        scratch_shapes=[pltpu.VMEM((nb, tc + 8, D_LRU), F32),
                        pltpu.VMEM((nl, nb * pitch, LANES), F32),
                        pltpu.VMEM((nl, nb * pitch, LANES), F32),
                        pltpu.VMEM((nl, nb * pitch, LANES), F32),
                        pltpu.VMEM((2 * ns, nb * pitch, LANES), F32),
                        pltpu.VMEM((2 * ns, nb * pitch, LANES), F32),
                        pltpu.VMEM((nl, nb, LANES), F32),
                        pltpu.VMEM((2 * ns, nb, LANES), F32)],
        compiler_params=_cparams(("arbitrary",)),
        name="rec_prompt",
    )(u, *params)


def _rec_sample_kernel(u_ref, h0_ref, buf0_ref, s0r_ref, s0i_ref,
                       cw_ref, cb_ref, wg_ref, bg_ref, c8_ref, lbr_ref, lbi_ref,
                       bbd_ref, cbd_ref, d_ref, wglu_ref, bglu_ref,
                       o_ref, hl_ref, cbuf_ref, sre_ref, sim_ref, *, nt):
    ns = S5_P // LANES
    ext = [buf0_ref[j] for j in range(CONV_W - 1)] + [u_ref[t, :, 0:D_LRU] for t in range(nt)]
    xcs = []
    for t in range(nt):
        xc = cb_ref[...]
        for j in range(CONV_W):
            xc = xc + ext[t + j] * cw_ref[j:j + 1, :]
        xcs.append(xc)
    a, bb = _lru_gates(jnp.concatenate(xcs, axis=0), wg_ref, bg_ref, c8_ref)
    nbat = h0_ref.shape[0]
    h = h0_ref[...]
    for t in range(nt):
        h = a[t * nbat:(t + 1) * nbat] * h + bb[t * nbat:(t + 1) * nbat]
        o_ref[t, :, 0:D_LRU] = h.astype(o_ref.dtype)
    hl_ref[...] = h
    for j in range(CONV_W - 1):
        cbuf_ref[j] = ext[nt + j]

    sx = jnp.concatenate([u_ref[t, :, D_LRU:U_COLS] for t in range(nt)], axis=0)
    bu = jnp.dot(sx.astype(BF16), bbd_ref[...], preferred_element_type=F32)
    lr = jnp.concatenate([lbr_ref[j] for j in range(ns)], axis=1)
    li = jnp.concatenate([lbi_ref[j] for j in range(ns)], axis=1)
    xr = s0r_ref[...]
    xi = s0i_ref[...]
    xs = []
    for t in range(nt):
        but = bu[t * nbat:(t + 1) * nbat]
        bur = jnp.concatenate([but[:, 2 * j * LANES:(2 * j + 1) * LANES] for j in range(ns)], axis=1)
        bui = jnp.concatenate([but[:, (2 * j + 1) * LANES:(2 * j + 2) * LANES] for j in range(ns)], axis=1)
        xr, xi = lr * xr - li * xi + bur, lr * xi + li * xr + bui
        xs.append(jnp.concatenate([xr, xi], axis=1).astype(BF16))
    sre_ref[...] = xr
    sim_ref[...] = xi
    s_out = _s5_out(jnp.concatenate(xs, axis=0), sx, cbd_ref, d_ref, wglu_ref, bglu_ref)
    for t in range(nt):
        o_ref[t, :, D_LRU:U_COLS] = s_out[t * nbat:(t + 1) * nbat].astype(o_ref.dtype)


def _rec_sample(u_t, h0, buf0_t, s0r, s0i, pr):
    nt, nbat, _ = u_t.shape
    params = [pr["cw"], pr["cb"], pr["wg"], pr["bg"], pr["c8"], pr["lbr"], pr["lbi"],
              pr["bbd"], pr["cbd"], pr["d"], pr["wglu"], pr["bglu"]]
    return pl.pallas_call(
        functools.partial(_rec_sample_kernel, nt=nt),
        out_shape=[jax.ShapeDtypeStruct((nt, nbat, U_COLS), BF16),
                   jax.ShapeDtypeStruct((nbat, D_LRU), F32),
                   jax.ShapeDtypeStruct((CONV_W - 1, nbat, D_LRU), F32),
                   jax.ShapeDtypeStruct((nbat, S5_P), F32),
                   jax.ShapeDtypeStruct((nbat, S5_P), F32)],
        compiler_params=pltpu.CompilerParams(vmem_limit_bytes=VMEM_LIMIT),
        name="rec_sample",
    )(u_t, h0, buf0_t, s0r, s0i, *params)


def _rec_params(lru_conv_w, lru_conv_b, lru_w_r, lru_b_r, lru_w_i, lru_b_i, lru_lam,
                s5_a_re, s5_a_im, s5_log_dt, s5_b_re, s5_b_im, s5_c_re, s5_c_im, s5_d, s5_w_glu, s5_b_glu):
    ns = S5_P // LANES
    bd = lambda w: jax.scipy.linalg.block_diag(*[w[n] for n in range(w.shape[0])])
    wg = jnp.concatenate([bd(lru_w_r), bd(lru_w_i)], axis=1).astype(BF16)
    bg = jnp.concatenate([lru_b_r, lru_b_i]).reshape(1, 2 * D_LRU)
    c8 = (LRU_C * jax.nn.log_sigmoid(lru_lam.astype(F32))).reshape(1, D_LRU)
    dt = jnp.exp(s5_log_dt.astype(F32))[:, None]
    ar = s5_a_re.astype(F32)
    ai = s5_a_im.astype(F32)
    mag = jnp.exp(ar * dt)
    lb_re = mag * jnp.cos(ai * dt)
    lb_im = mag * jnp.sin(ai * dt)
    den = ar * ar + ai * ai
    n_re = lb_re - 1.0
    co_re = (n_re * ar + lb_im * ai) / den
    co_im = (lb_im * ar - n_re * ai) / den
    bb_re = co_re[..., None] * s5_b_re - co_im[..., None] * s5_b_im
    bb_im = co_re[..., None] * s5_b_im + co_im[..., None] * s5_b_re
    bre = bd(jnp.swapaxes(bb_re, 1, 2))
    bim = bd(jnp.swapaxes(bb_im, 1, 2))
    bbd = jnp.stack([bre.reshape(D_S5, ns, LANES), bim.reshape(D_S5, ns, LANES)], axis=2)
    bbd = bbd.reshape(D_S5, 2 * S5_P).astype(BF16)
    cre = bd(jnp.swapaxes(s5_c_re, 1, 2))
    cim = bd(jnp.swapaxes(s5_c_im, 1, 2))
    cbd = jnp.concatenate([cre, -cim], axis=0).astype(BF16)
    return dict(cw=lru_conv_w, cb=lru_conv_b.reshape(1, D_LRU), wg=wg, bg=bg, c8=c8,
                lbr=lb_re.reshape(ns, 1, LANES), lbi=lb_im.reshape(ns, 1, LANES),
                bbd=bbd, cbd=cbd, d=s5_d.reshape(1, D_S5),
                wglu=s5_w_glu.astype(BF16), bglu=s5_b_glu.reshape(1, D_S5))


def _moba_prompt_kernel(pt_ref, qt_ref, krm_ref, kt_ref, vt_ref, cache_ref, o_ref, ks_ref,
                        km_s, vb_s, bias_s, sc_s, pbuf, psem, *, s, nblk, grp, nh, layer, pg, nsteps):
    qi = pl.program_id(2)
    step = (pl.program_id(0) * (N_HEADS // nh) + pl.program_id(1)) * nblk + qi
    tq = MOBA_BLOCK
    hd = HEAD_DIM
    gk = grp * tq

    @pl.when(qi == 0)
    def _():
        r = lax.broadcasted_iota(jnp.int32, (nblk, s), 1) // MOBA_BLOCK
        n = lax.broadcasted_iota(jnp.int32, (nblk, s), 0)
        pm = jnp.where(r == n, 1.0 / MOBA_BLOCK, 0.0).astype(F32)
        for hh in range(nh):
            km_s[hh] = lax.dot_general(pm, kt_ref[0, hh * hd:(hh + 1) * hd, :], NT_DIMS,
                                       precision=HIGHEST, preferred_element_type=F32)
        ones_row = (lax.broadcasted_iota(jnp.int32, (VB_ROWS - hd, gk), 0) == 0).astype(BF16)
        for g in range(nblk // grp):
            for hh in range(nh):
                vb_s[g, hh, 0:hd, :] = vt_ref[0, hh * hd:(hh + 1) * hd, g * gk:(g + 1) * gk].astype(BF16)
                vb_s[g, hh, hd:VB_ROWS, :] = ones_row

    qt = qt_ref[0]
    qs = (qt * (hd ** -0.5 * LOG2E)).astype(BF16)
    feat = lax.broadcasted_iota(jnp.int32, (nh * hd, tq), 0)
    blk_iota = lax.broadcasted_iota(jnp.int32, (nblk, tq), 0)
    past = blk_iota < qi
    ws = []
    for hh in range(nh):
        ws.append(jnp.where((feat >= hh * hd) & (feat < (hh + 1) * hd), qs, jnp.zeros_like(qs)))
        sb = jnp.dot(km_s[hh], qt[hh * hd:(hh + 1) * hd, :], precision=HIGHEST,
                     preferred_element_type=F32)
        sb = jnp.where(past, sb, -jnp.inf)
        cnt = jnp.zeros((nblk, tq), jnp.int32)
        for i in range(nblk):
            ri = sb[i:i + 1, :]
            beats = (ri > sb) | ((ri == sb) & (i < blk_iota))
            cnt = cnt + beats.astype(jnp.int32)
        sel = (cnt < MOBA_TOPK) & past
        bias_s[hh] = jnp.where(sel | (blk_iota == qi), 0.0, NEG).astype(F32)

    def scores(g, hh):
        st = pl.multiple_of(g * gk, gk)
        sc = jnp.dot(krm_ref[0, pl.ds(st, gk), :], ws[hh], preferred_element_type=F32)
        return jnp.concatenate(
            [sc[i * tq:(i + 1) * tq] + bias_s[hh, pl.ds(g * grp + i, 1), :] for i in range(grp)], axis=0)

    def pass1(g, ms):
        st = pl.multiple_of(g * gk, gk)
        out = []
        for hh in range(nh):
            sc = scores(g, hh)
            sc_s[hh, pl.ds(st, gk), :] = sc
            out.append(jnp.maximum(ms[hh], jnp.max(sc, axis=0, keepdims=True)))
        return tuple(out)

    g_own = qi // grp
    ms = lax.fori_loop(0, g_own, pass1, tuple(jnp.full((1, tq), NEG, F32) for _ in range(nh)))

    _ksum_step(pt_ref, cache_ref, ks_ref, pbuf, psem, step, layer=layer, nsteps=nsteps, pg=pg)

    key = lax.broadcasted_iota(jnp.int32, (gk, tq), 0)
    qry = lax.broadcasted_iota(jnp.int32, (gk, tq), 1)
    causal = key <= qry + (qi * tq - g_own * gk)
    st_own = pl.multiple_of(g_own * gk, gk)
    ms = list(ms)
    for hh in range(nh):
        sc = jnp.where(causal, scores(g_own, hh), NEG)
        sc_s[hh, pl.ds(st_own, gk), :] = sc
        ms[hh] = jnp.maximum(ms[hh], jnp.max(sc, axis=0, keepdims=True))

    def pass2(g, accs):
        st = pl.multiple_of(g * gk, gk)
        out = []
        for hh in range(nh):
            p = jnp.exp2(sc_s[hh, pl.ds(st, gk), :] - ms[hh])
            out.append(accs[hh] + jnp.dot(vb_s[g, hh], p.astype(BF16), preferred_element_type=F32))
        return tuple(out)

    accs = lax.fori_loop(0, g_own + 1, pass2, tuple(jnp.zeros((VB_ROWS, tq), F32) for _ in range(nh)))
    out_t = jnp.concatenate([acc[0:hd] / acc[hd:hd + 1] for acc in accs], axis=0)
    o_ref[0] = jnp.transpose(out_t).astype(o_ref.dtype)


def _moba_prompt(qt, krm, kt, vt, pt_flat, cache_kt, layer):
    nb, _, s = qt.shape
    nblk = s // MOBA_BLOCK
    tq = MOBA_BLOCK
    nh = 4
    grp = 4
    hp = nh * HEAD_DIM
    nhp = N_HEADS // nh
    nsteps = nb * nhp * nblk
    pg = pt_flat.shape[0] // nsteps
    nbat = pt_flat.shape[0] // N_PAGES
    steps_per_b = N_PAGES // pg
    once = dict(pipeline_mode=pl.Buffered(1))
    return pl.pallas_call(
        functools.partial(_moba_prompt_kernel, s=s, nblk=nblk, grp=grp, nh=nh, layer=layer, pg=pg,
                          nsteps=nsteps),
        grid_spec=pltpu.PrefetchScalarGridSpec(
            num_scalar_prefetch=1,
            grid=(nb, nhp, nblk),
            in_specs=[pl.BlockSpec((1, hp, tq), lambda b, h, i, pt: (b, h, i)),
                      pl.BlockSpec((1, s, hp), lambda b, h, i, pt: (b, 0, h), **once),
                      pl.BlockSpec((1, hp, s), lambda b, h, i, pt: (b, h, 0), **once),
                      pl.BlockSpec((1, hp, s), lambda b, h, i, pt: (b, h, 0), **once),
                      pl.BlockSpec(memory_space=pl.ANY)],
            out_specs=[pl.BlockSpec((1, tq, hp), lambda b, h, i, pt: (b, i, h)),
                       pl.BlockSpec((1, N_HEADS, HEAD_DIM, LANES),
                                    lambda b, h, i, pt: (((b * nhp + h) * nblk + i) // steps_per_b, 0, 0, 0))],
            scratch_shapes=[pltpu.VMEM((nh, nblk, HEAD_DIM), F32),
                            pltpu.VMEM((nblk // grp, nh, VB_ROWS, grp * tq), BF16),
                            pltpu.VMEM((nh, nblk, tq), F32),
                            pltpu.VMEM((nh, s, tq), F32),
                            pltpu.VMEM((2, pg, N_HEADS, HEAD_DIM, PAGE_SIZE), F32),
                            pltpu.SemaphoreType.DMA((2, pg))]),
        out_shape=[jax.ShapeDtypeStruct((nb, s, D_ATT), BF16),
                   jax.ShapeDtypeStruct((nbat, N_HEADS, HEAD_DIM, LANES), F32)],
        compiler_params=_cparams(("arbitrary", "arbitrary", "arbitrary")),
        name="moba_prompt",
    )(pt_flat, qt, krm, kt, vt, cache_kt)


PAGES_PER_BLOCK = MOBA_BLOCK // PAGE_SIZE
N_PAGES = PAST_LEN // PAGE_SIZE
N_FULL = PAST_LEN // MOBA_BLOCK
SAMPLE_SLOTS = 3


def _ksum_step(pt_ref, cache_ref, o_ref, buf, sem, n, *, layer, nsteps, pg):
    slot = n % 2
    steps_per_b = N_PAGES // pg
    g = n % steps_per_b
    bpg = pg // PAGES_PER_BLOCK

    def start(step, sl):
        for j in range(pg):
            pid = pt_ref[step * pg + j]
            pltpu.make_async_copy(cache_ref.at[layer, pid], buf.at[sl, j], sem.at[sl, j]).start()

    @pl.when(n == 0)
    def _():
        start(0, 0)

    @pl.when(n + 1 < nsteps)
    def _():
        start(n + 1, 1 - slot)

    @pl.when(g == 0)
    def _():
        o_ref[...] = jnp.zeros(o_ref.shape, F32)

    for j in range(pg):
        pltpu.make_async_copy(buf.at[slot, j], buf.at[slot, j], sem.at[slot, j]).wait()
    lane = lax.broadcasted_iota(jnp.int32, (HEAD_DIM, LANES), 1)
    for hh in range(N_HEADS):
        acc = o_ref[0, hh]
        for blk in range(bpg):
            tile = buf[slot, PAGES_PER_BLOCK * blk, hh]
            for pp in range(1, PAGES_PER_BLOCK):
                tile = tile + buf[slot, PAGES_PER_BLOCK * blk + pp, hh]
            acc = jnp.where(lane == g * bpg + blk, jnp.sum(tile, axis=1, keepdims=True), acc)
        o_ref[0, hh] = acc


def _topk_kernel(q_ref, ks_ref, o_ref):
    nq = q_ref.shape[2]
    lane = lax.broadcasted_iota(jnp.int32, (nq, LANES), 1)
    for hh in range(N_HEADS):
        km = ks_ref[0, hh] * (1.0 / MOBA_BLOCK)
        sb = jnp.dot(q_ref[0, hh], km, precision=HIGHEST, preferred_element_type=F32)
        sb = jnp.where(lane < N_FULL, sb, -jnp.inf)
        out = jnp.zeros((nq, LANES), jnp.int32)
        for r in range(MOBA_TOPK):
            mx = jnp.max(sb, axis=1, keepdims=True)
            idx = jnp.min(jnp.where(sb == mx, lane, LANES), axis=1, keepdims=True)
            out = jnp.where(lane == r, idx, out)
            sb = jnp.where(lane == idx, -jnp.inf, sb)
        o_ref[0, hh] = out


def _sample_topk(q_pad, ksum_t):
    nbat, nh, nq, hd = q_pad.shape
    return pl.pallas_call(
        _topk_kernel,
        grid=(nbat,),
        in_specs=[pl.BlockSpec((1, nh, nq, hd), lambda b: (b, 0, 0, 0)),
                  pl.BlockSpec((1, nh, hd, LANES), lambda b: (b, 0, 0, 0))],
        out_specs=pl.BlockSpec((1, nh, nq, LANES), lambda b: (b, 0, 0, 0)),
        out_shape=jax.ShapeDtypeStruct((nbat, nh, nq, LANES), jnp.int32),
        compiler_params=_cparams(("arbitrary",)),
        name="sample_topk",
    )(q_pad, ksum_t)


def _slab_copy(src, buf, sem, slot, j):
    return pltpu.make_async_copy(src, buf.at[slot, j], sem.at[slot, j])


def _moba_sample_kernel(idx_ref, pt_ref, q_ref, kn_ref, vn_ref, ck_ref, cv_ref, o_ref,
                        kbuf, vbuf, ksem, vsem, *, layer, nsteps, nt):
    n = pl.program_id(0)
    slot = n % SAMPLE_SLOTS
    b = n // N_HEADS
    nsel = MOBA_TOPK * PAGES_PER_BLOCK
    ahead = SAMPLE_SLOTS - 1

    def start(step, sl):
        sb = step // N_HEADS
        sh = step % N_HEADS
        for t in range(nt):
            for r in range(MOBA_TOPK):
                blk = idx_ref[(step * nt + t) * MOBA_TOPK + r]
                for pp in range(PAGES_PER_BLOCK):
                    pid = pt_ref[sb * N_PAGES + blk * PAGES_PER_BLOCK + pp]
                    j = (t * MOBA_TOPK + r) * PAGES_PER_BLOCK + pp
                    _slab_copy(ck_ref.at[layer, pid, sh], kbuf, ksem, sl, j).start()
                    _slab_copy(cv_ref.at[layer, pid, sh], vbuf, vsem, sl, j).start()

    def wait(sl):
        for j in range(nt * nsel):
            _slab_copy(kbuf.at[sl, j], kbuf, ksem, sl, j).wait()
            _slab_copy(vbuf.at[sl, j], vbuf, vsem, sl, j).wait()

    @pl.when(n == 0)
    def _():
        for s0 in range(ahead):
            start(s0, s0)

    @pl.when(n + ahead < nsteps)
    def _():
        start(n + ahead, (n + ahead) % SAMPLE_SLOTS)

    wait(slot)

    qs = (q_ref[0, 0] * (HEAD_DIM ** -0.5)).astype(BF16)
    nq = qs.shape[0]
    nkeys = nsel * PAGE_SIZE
    qrow = lax.broadcasted_iota(jnp.int32, (nq, nkeys), 0)
    scores = []
    for t in range(nt):
        kt = jnp.concatenate([kbuf[slot, t * nsel + i] for i in range(nsel)], axis=1).astype(BF16)
        st = jnp.dot(qs, kt, preferred_element_type=F32)
        scores.append(jnp.where(qrow == t, st, -jnp.inf))
    ntok = kn_ref.shape[2]
    so = jnp.dot(qs, kn_ref[0].astype(BF16), preferred_element_type=F32)
    orow = lax.broadcasted_iota(jnp.int32, (nq, ntok), 0)
    opos = lax.broadcasted_iota(jnp.int32, (nq, ntok), 1) - b * nt
    so = jnp.where((opos >= 0) & (opos <= orow), so, -jnp.inf)
    m = jnp.max(so, axis=1, keepdims=True)
    for st in scores:
        m = jnp.maximum(m, jnp.max(st, axis=1, keepdims=True))
    po = jnp.exp(so - m)
    den = jnp.sum(po, axis=1, keepdims=True)
    acc = lax.dot_general(po.astype(BF16), vn_ref[0].astype(BF16), NT_DIMS, preferred_element_type=F32)
    for t in range(nt):
        pt_ = jnp.exp(scores[t] - m)
        den = den + jnp.sum(pt_, axis=1, keepdims=True)
        vt = jnp.concatenate([vbuf[slot, t * nsel + i] for i in range(nsel)], axis=1).astype(BF16)
        acc = acc + lax.dot_general(pt_.astype(BF16), vt, NT_DIMS, preferred_element_type=F32)
    o_ref[0, 0] = acc / den


def _moba_sample(q_pad, kt_new, vt_new, cache_kt, cache_vt, idx_flat, pt_flat, layer, nt):
    nbat, nh, nq, hd = q_pad.shape
    ntok = kt_new.shape[1]
    nsteps = nbat * nh
    nslab = nt * MOBA_TOPK * PAGES_PER_BLOCK
    qtok = pl.BlockSpec((1, 1, nq, hd), lambda n, idx, pt: (n // nh, n % nh, 0, 0))
    new = pl.BlockSpec((1, hd, ntok), lambda n, idx, pt: (n % nh, 0, 0))
    return pl.pallas_call(
        functools.partial(_moba_sample_kernel, layer=layer, nsteps=nsteps, nt=nt),
        grid_spec=pltpu.PrefetchScalarGridSpec(
            num_scalar_prefetch=2,
            grid=(nsteps,),
            in_specs=[qtok, new, new, pl.BlockSpec(memory_space=pl.ANY), pl.BlockSpec(memory_space=pl.ANY)],
            out_specs=qtok,
            scratch_shapes=[pltpu.VMEM((SAMPLE_SLOTS, nslab, hd, PAGE_SIZE), F32),
                            pltpu.VMEM((SAMPLE_SLOTS, nslab, hd, PAGE_SIZE), F32),
                            pltpu.SemaphoreType.DMA((SAMPLE_SLOTS, nslab)),
                            pltpu.SemaphoreType.DMA((SAMPLE_SLOTS, nslab))]),
        out_shape=jax.ShapeDtypeStruct((nbat, nh, nq, hd), F32),
        compiler_params=_cparams(("arbitrary",)),
        name="moba_sample",
    )(idx_flat, pt_flat, q_pad, kt_new.reshape(nh, hd, ntok), vt_new.reshape(nh, hd, ntok),
      cache_kt, cache_vt)


def _rope_tables(pos):
    inv = jnp.power(ROPE_THETA, -jnp.arange(HALF, dtype=F32) / HALF)
    ang = pos.astype(F32)[:, None] * inv[None, :]
    cos, sin = jnp.cos(ang), jnp.sin(ang)
    reps = LANES // HEAD_DIM
    return (jnp.tile(jnp.concatenate([cos, cos], axis=1), (1, reps)),
            jnp.tile(jnp.concatenate([-sin, sin], axis=1), (1, reps)),
            cos.T, sin.T)


def _split_w_in(w):
    a_x, a_g, q, k, v, b_g, s_x, s_g = jnp.split(
        w, (D_LRU, 2 * D_LRU, 2 * D_LRU + D_ATT, 2 * D_LRU + 2 * D_ATT, 2 * D_LRU + 3 * D_ATT,
            2 * D_LRU + 4 * D_ATT, 2 * D_LRU + 4 * D_ATT + D_S5), axis=1)
    w_row = jnp.concatenate([a_x, s_x, a_g, b_g, s_g, k], axis=1).astype(BF16)
    w_t = jnp.concatenate([q, k, v], axis=1).T.astype(BF16)
    return w_row, w_t


def kernel(x_prompt, x_sample, cache_k, cache_v, state_lru_h, state_lru_conv, state_s5_re, state_s5_im,
           page_table, c_prompt, c_sample, norm_g, w_ada, b_ada, w_in, w_out, lru_conv_w, lru_conv_b,
           lru_w_r, lru_b_r, lru_w_i, lru_b_i, lru_lam, s5_a_re, s5_a_im, s5_log_dt, s5_b_re, s5_b_im,
           s5_c_re, s5_c_im, s5_d, s5_w_glu, s5_b_glu, final_g):
    nbp, seq, _ = x_prompt.shape
    nbs, nts, _ = x_sample.shape
    ntok_s = nbs * nts

    rope_p = _rope_tables(jnp.arange(seq, dtype=jnp.int32))
    rope_s = _rope_tables(PAST_LEN + (jnp.arange(ntok_s, dtype=jnp.int32) % nts))

    n_c = nbp + nbs
    c_all = jnp.concatenate([c_prompt, c_sample, jnp.zeros((-n_c % 8, D_MODEL), F32)], axis=0)
    mod = _modulation(c_all, w_ada, b_ada)

    cache_kt = jnp.swapaxes(cache_k, 3, 4)
    cache_vt = jnp.swapaxes(cache_v, 3, 4)
    pt_flat = page_table.reshape(-1)
    xp = x_prompt
    xs = x_sample.reshape(1, ntok_s, D_MODEL)
    outs_p = [[] for _ in range(6)]
    outs_s = [[] for _ in range(6)]
    for l in range(DEPTH):
        last = l == DEPTH - 1
        w_row, w_t = _split_w_in(w_in[l])
        w_o = w_out[l].astype(BF16)
        pr = _rec_params(lru_conv_w[l], lru_conv_b[l], lru_w_r[l], lru_b_r[l], lru_w_i[l], lru_b_i[l],
                         lru_lam[l], s5_a_re[l], s5_a_im[l], s5_log_dt[l], s5_b_re[l], s5_b_im[l],
                         s5_c_re[l], s5_c_im[l], s5_d[l], s5_w_glu[l], s5_b_glu[l])
        shift, scale, gate = jnp.split(mod[l], 3, axis=-1)

        mp = lambda a: a[:nbp].reshape(nbp, 1, D_MODEL)
        u, gates, krm, qt, kt, vt = _inproj(xp, mp(scale), mp(shift), norm_g[l], w_row, w_t, rope_p, tm=512)
        as_out, h_last, cbuf, s_re, s_im = _rec_prompt(u, pr, tc=256)
        att, ksum_t = _moba_prompt(qt, krm, kt, vt, pt_flat, cache_kt, l)
        xp = _outproj(xp, as_out, att, gates, mp(gate), w_o, final_g, tm=512, final=last)
        for lst, val in zip(outs_p, (kt.reshape(nbp, N_HEADS, HEAD_DIM, seq), vt.reshape(nbp, N_HEADS, HEAD_DIM, seq),
                                     h_last, cbuf, s_re.reshape(nbp, S5_GROUPS, S5_STATE),
                                     s_im.reshape(nbp, S5_GROUPS, S5_STATE))):
            lst.append(val)

        ms = lambda a: jnp.repeat(a[nbp:n_c], nts, axis=0).reshape(1, ntok_s, D_MODEL)
        u, gates, _, qt, kt, vt = _inproj(xs, ms(scale), ms(shift), norm_g[l], w_row, w_t, rope_s, tm=ntok_s)
        to_heads = lambda a: a.reshape(N_HEADS, HEAD_DIM, nbs, nts).transpose(2, 0, 3, 1)
        q, k, v = to_heads(qt), to_heads(kt), to_heads(vt)
        u_t = u.reshape(nbs, nts, U_COLS).transpose(1, 0, 2)
        as_t, h_last, cbuf_t, s_re, s_im = _rec_sample(
            u_t, state_lru_h[l], state_lru_conv[l].transpose(1, 0, 2),
            state_s5_re[l].reshape(nbs, S5_P), state_s5_im[l].reshape(nbs, S5_P), pr)
        as_out = as_t.transpose(1, 0, 2).reshape(1, ntok_s, U_COLS)
        q_pad = jnp.pad(q, ((0, 0), (0, 0), (0, -nts % 8), (0, 0)))
        idx = _sample_topk(q_pad, ksum_t)[:, :, :nts, :MOBA_TOPK].reshape(-1)
        att = _moba_sample(q_pad, kt[0], vt[0], cache_kt, cache_vt, idx, pt_flat, l, nts)[:, :, :nts]
        att = att.transpose(0, 2, 1, 3).reshape(1, ntok_s, D_ATT).astype(BF16)
        xs = _outproj(xs, as_out, att, gates, ms(gate), w_o, final_g, tm=ntok_s, final=last)
        for lst, val in zip(outs_s, (k, v, h_last, cbuf_t.transpose(1, 0, 2),
                                     s_re.reshape(nbs, S5_GROUPS, S5_STATE),
                                     s_im.reshape(nbs, S5_GROUPS, S5_STATE))):
            lst.append(val)

    k_p, v_p = (jnp.swapaxes(jnp.stack(o), 3, 4) for o in outs_p[:2])
    return (xp, xs.reshape(nbs, nts, D_MODEL), k_p, v_p,
            *[jnp.stack(o) for o in outs_p[2:]], *[jnp.stack(o) for o in outs_s])
```

```python
import functools

import jax
import jax.numpy as jnp
from jax import lax
from jax.experimental import pallas as pl
from jax.experimental.pallas import tpu as pltpu

F32 = jnp.float32
BF16 = jnp.bfloat16
HIGHEST = lax.Precision.HIGHEST

D_MODEL = 1024
DEPTH = 2
PAST_LEN = 16384
PAGE_SIZE = 128
D_LRU = 256
LRU_BLOCKS = 4
LRU_BW = D_LRU // LRU_BLOCKS
CONV_W = 4
LRU_C = 8.0
N_HEADS = 8
HEAD_DIM = 64
HALF = HEAD_DIM // 2
D_ATT = N_HEADS * HEAD_DIM
MOBA_BLOCK = 256
MOBA_TOPK = 3
ROPE_THETA = 10000.0
D_S5 = 256
S5_GROUP = 16
S5_GROUPS = D_S5 // S5_GROUP
S5_STATE = 64
S5_P = S5_GROUPS * S5_STATE
D_MIX = D_LRU + D_ATT + D_S5
EPS = 1e-6

LANES = 128
BF16_ROWS = 16
NEG = -1e30
LOG2E = 1.4426950408889634
VB_ROWS = HEAD_DIM + BF16_ROWS

U_COLS = D_LRU + D_S5
G_COLS = D_MIX
OFF_U = 0
OFF_G = OFF_U + U_COLS
OFF_KR = OFF_G + G_COLS

VMEM_LIMIT = 56 * 1024 * 1024

NT_DIMS = (((1,), (1,)), ((), ()))


def _cparams(sem):
    return pltpu.CompilerParams(dimension_semantics=sem, vmem_limit_bytes=VMEM_LIMIT)


def _silu(x):
    return x * jax.nn.sigmoid(x)


def _mod_kernel(c_ref, w_ref, b_ref, o_ref):
    o_ref[0] = jnp.dot(_silu(c_ref[...]), w_ref[0], precision=HIGHEST,
                       preferred_element_type=F32) + b_ref[0]


def _modulation(c_all, w_ada, b_ada):
    n = c_all.shape[0]
    tn = D_MODEL
    return pl.pallas_call(
        _mod_kernel,
        grid=(DEPTH, 3 * D_MODEL // tn),
        in_specs=[pl.BlockSpec((n, D_MODEL), lambda l, j: (0, 0)),
                  pl.BlockSpec((1, D_MODEL, tn), lambda l, j: (l, 0, j)),
                  pl.BlockSpec((1, 1, tn), lambda l, j: (l, 0, j))],
        out_specs=pl.BlockSpec((1, n, tn), lambda l, j: (l, 0, j)),
        out_shape=jax.ShapeDtypeStruct((DEPTH, n, 3 * D_MODEL), F32),
        compiler_params=_cparams(("arbitrary", "arbitrary")),
        name="modulation",
    )(c_all, w_ada, b_ada.reshape(DEPTH, 1, 3 * D_MODEL))


def _inproj_kernel(x_ref, sc_ref, sh_ref, g_ref, w_ref, wt_ref, cos_ref, sin_ref, cost_ref, sint_ref,
                   u_ref, gate_ref, krm_ref, qt_ref, kt_ref, vt_ref):
    x = x_ref[0]
    h = x * lax.rsqrt(jnp.mean(x * x, axis=-1, keepdims=True) + EPS) * g_ref[...]
    h = h * (1.0 + sc_ref[0]) + sh_ref[0]
    hb = h.astype(BF16)

    def proj(lo, n):
        return jnp.dot(hb, w_ref[:, lo:lo + n], preferred_element_type=F32)

    u_ref[0] = proj(OFF_U, U_COLS)
    gate_ref[0] = _silu(proj(OFF_G, G_COLS)).astype(gate_ref.dtype)

    cos = cos_ref[...]
    sin = sin_ref[...]
    lane = lax.broadcasted_iota(jnp.int32, cos.shape, 1)
    first_half = (lane & (HEAD_DIM - 1)) < HALF
    p = proj(OFF_KR, D_ATT)
    for j in range(D_ATT // LANES):
        xs = p[:, j * LANES:(j + 1) * LANES]
        partner = jnp.where(first_half, pltpu.roll(xs, LANES - HALF, 1), pltpu.roll(xs, HALF, 1))
        krm_ref[0, :, j * LANES:(j + 1) * LANES] = (xs * cos + partner * sin).astype(BF16)

    cost = cost_ref[...]
    sint = sint_ref[...]
    for n, (ref, use_rope) in enumerate(((qt_ref, True), (kt_ref, True), (vt_ref, False))):
        pt = lax.dot_general(wt_ref[n * D_ATT:(n + 1) * D_ATT, :], hb, NT_DIMS,
                             preferred_element_type=F32)
        if not use_rope:
            ref[0] = pt
            continue
        for hh in range(N_HEADS):
            r0 = hh * HEAD_DIM
            x1 = pt[r0:r0 + HALF]
            x2 = pt[r0 + HALF:r0 + HEAD_DIM]
            ref[0, r0:r0 + HALF, :] = x1 * cost - x2 * sint
            ref[0, r0 + HALF:r0 + HEAD_DIM, :] = x2 * cost + x1 * sint


def _inproj(x, scale, shift, g, w_row, w_t, rope, tm):
    nb, s, _ = x.shape
    cos_r, sin_r, cos_t, sin_t = rope
    rm = scale.shape[1]
    mod_block = (1, tm, D_MODEL) if rm == s else (1, 1, D_MODEL)
    mod_map = (lambda b, i: (b, i, 0)) if rm == s else (lambda b, i: (b, 0, 0))
    fm_spec = pl.BlockSpec((1, D_ATT, tm), lambda b, i: (b, 0, i))
    fm_shape = jax.ShapeDtypeStruct((nb, D_ATT, s), F32)
    return pl.pallas_call(
        _inproj_kernel,
        grid=(nb, s // tm),
        in_specs=[pl.BlockSpec((1, tm, D_MODEL), lambda b, i: (b, i, 0)),
                  pl.BlockSpec(mod_block, mod_map),
                  pl.BlockSpec(mod_block, mod_map),
                  pl.BlockSpec((1, D_MODEL), lambda b, i: (0, 0)),
                  pl.BlockSpec(w_row.shape, lambda b, i: (0, 0)),
                  pl.BlockSpec(w_t.shape, lambda b, i: (0, 0)),
                  pl.BlockSpec((tm, LANES), lambda b, i: (i, 0)),
                  pl.BlockSpec((tm, LANES), lambda b, i: (i, 0)),
                  pl.BlockSpec((HALF, tm), lambda b, i: (0, i)),
                  pl.BlockSpec((HALF, tm), lambda b, i: (0, i))],
        out_specs=[pl.BlockSpec((1, tm, U_COLS), lambda b, i: (b, i, 0)),
                   pl.BlockSpec((1, tm, G_COLS), lambda b, i: (b, i, 0)),
                   pl.BlockSpec((1, tm, D_ATT), lambda b, i: (b, i, 0)),
                   fm_spec, fm_spec, fm_spec],
        out_shape=[jax.ShapeDtypeStruct((nb, s, U_COLS), F32),
                   jax.ShapeDtypeStruct((nb, s, G_COLS), BF16),
                   jax.ShapeDtypeStruct((nb, s, D_ATT), BF16),
                   fm_shape, fm_shape, fm_shape],
        compiler_params=_cparams(("arbitrary", "arbitrary")),
        name="inproj",
    )(x, scale, shift, g.reshape(1, D_MODEL), w_row, w_t, cos_r, sin_r, cos_t, sin_t)


def _outproj_kernel(x_ref, as_ref, att_ref, gate_ref, gm_ref, w_ref, fg_ref, o_ref, *, final):
    g = gate_ref[0].astype(F32)
    a_s = as_ref[0].astype(F32)
    mixed = jnp.concatenate([a_s[:, :D_LRU] * g[:, :D_LRU],
                             att_ref[0].astype(F32) * g[:, D_LRU:D_LRU + D_ATT],
                             a_s[:, D_LRU:] * g[:, D_LRU + D_ATT:]], axis=1).astype(BF16)
    y = jnp.dot(mixed, w_ref[...], preferred_element_type=F32)
    xn = x_ref[0] + gm_ref[0] * y
    if final:
        xn = xn * lax.rsqrt(jnp.mean(xn * xn, axis=-1, keepdims=True) + EPS) * fg_ref[...]
    o_ref[0] = xn


def _outproj(x, as_out, att, gates, gate_mod, w_out, final_g, tm, final):
    nb, s, _ = x.shape
    rm = gate_mod.shape[1]
    mod_block = (1, tm, D_MODEL) if rm == s else (1, 1, D_MODEL)
    mod_map = (lambda b, i: (b, i, 0)) if rm == s else (lambda b, i: (b, 0, 0))
    row = lambda n: pl.BlockSpec((1, tm, n), lambda b, i: (b, i, 0))
    return pl.pallas_call(
        functools.partial(_outproj_kernel, final=final),
        grid=(nb, s // tm),
        in_specs=[row(D_MODEL), row(U_COLS), row(D_ATT), row(G_COLS),
                  pl.BlockSpec(mod_block, mod_map),
                  pl.BlockSpec((D_MIX, D_MODEL), lambda b, i: (0, 0)),
                  pl.BlockSpec((1, D_MODEL), lambda b, i: (0, 0))],
        out_specs=row(D_MODEL),
        out_shape=jax.ShapeDtypeStruct((nb, s, D_MODEL), F32),
        compiler_params=_cparams(("arbitrary", "arbitrary")),
        name="outproj",
    )(x, as_out, att, gates, gate_mod, w_out, final_g.reshape(1, D_MODEL))


def _lru_gates(xc, wg_ref, bg_ref, c8_ref):
    gates = jnp.dot(xc.astype(BF16), wg_ref[...], preferred_element_type=F32) + bg_ref[...]
    r = jax.nn.sigmoid(gates[:, :D_LRU])
    i = jax.nn.sigmoid(gates[:, D_LRU:])
    log_a = r * c8_ref[...]
    a = jnp.exp(log_a)
    mult = jnp.sqrt(1.0 - jnp.exp(2.0 * log_a))
    return a, mult * i * xc


def _s5_out(x_cat_bf16, sx, cbd_ref, d_ref, wglu_ref, bglu_ref):
    y = jnp.dot(x_cat_bf16, cbd_ref[...], preferred_element_type=F32) + d_ref[...] * sx
    z = jax.nn.gelu(y)
    return z * jax.nn.sigmoid(jnp.dot(z.astype(BF16), wglu_ref[...], preferred_element_type=F32)
                              + bglu_ref[...])


def _rec_prompt_kernel(u_ref, cw_ref, cb_ref, wg_ref, bg_ref, c8_ref, lbr_ref, lbi_ref,
                       bbd_ref, cbd_ref, d_ref, wglu_ref, bglu_ref,
                       o_ref, hl_ref, cbuf_ref, sre_ref, sim_ref,
                       ext, a_s, b_s, h_s, bu_s, x_s, h_st, x_st, *, nb, tc, pitch):
    c = pl.program_id(0)
    nl = D_LRU // LANES
    ns = S5_P // LANES

    @pl.when(c == 0)
    def _():
        ext[:, 0:8, :] = jnp.zeros((nb, 8, D_LRU), F32)
        h_st[...] = jnp.zeros(h_st.shape, F32)
        x_st[...] = jnp.zeros(x_st.shape, F32)

    for b in range(nb):
        r0 = b * pitch
        ext[b, 8:8 + tc, :] = u_ref[b, :, 0:D_LRU]
        xc = cb_ref[...]
        for j in range(CONV_W):
            xc = xc + ext[b, 8 - (CONV_W - 1) + j:8 - (CONV_W - 1) + j + tc, :] * cw_ref[j:j + 1, :]
        ext[b, 8 - (CONV_W - 1):8, :] = ext[b, 8 + tc - (CONV_W - 1):8 + tc, :]
        a, bb = _lru_gates(xc, wg_ref, bg_ref, c8_ref)
        for j in range(nl):
            a_s[j, r0:r0 + tc, :] = a[:, j * LANES:(j + 1) * LANES]
            b_s[j, r0:r0 + tc, :] = bb[:, j * LANES:(j + 1) * LANES]
        sxb = u_ref[b, :, D_LRU:U_COLS].astype(BF16)
        for j in range(ns):
            bu = jnp.dot(sxb, bbd_ref[:, 2 * j * LANES:(2 * j + 2) * LANES], preferred_element_type=F32)
            bu_s[j, r0:r0 + tc, :] = bu[:, :LANES]
            bu_s[ns + j, r0:r0 + tc, :] = bu[:, LANES:]

    def step(t, carry):
        h, xr, xi = carry
        rows = pl.ds(t, nb, stride=pitch)
        h_new, xr_new, xi_new = [], [], []
        for j in range(nl):
            hj = a_s[j, rows, :] * h[j] + b_s[j, rows, :]
            h_s[j, rows, :] = hj
            h_new.append(hj)
        for j in range(ns):
            lr = lbr_ref[j]
            li = lbi_ref[j]
            nr = lr * xr[j] - li * xi[j] + bu_s[j, rows, :]
            ni = lr * xi[j] + li * xr[j] + bu_s[ns + j, rows, :]
            x_s[j, rows, :] = nr
            x_s[ns + j, rows, :] = ni
            xr_new.append(nr)
            xi_new.append(ni)
        return tuple(h_new), tuple(xr_new), tuple(xi_new)

    init = (tuple(h_st[j] for j in range(nl)),
            tuple(x_st[j] for j in range(ns)),
            tuple(x_st[ns + j] for j in range(ns)))
    h, xr, xi = lax.fori_loop(0, tc, step, init)
    for j in range(nl):
        h_st[j] = h[j]
        hl_ref[:, j * LANES:(j + 1) * LANES] = h[j]
    for j in range(ns):
        x_st[j] = xr[j]
        x_st[ns + j] = xi[j]
        sre_ref[:, j * LANES:(j + 1) * LANES] = xr[j]
        sim_ref[:, j * LANES:(j + 1) * LANES] = xi[j]

    for b in range(nb):
        r0 = b * pitch
        for j in range(nl):
            o_ref[b, :, j * LANES:(j + 1) * LANES] = h_s[j, r0:r0 + tc, :].astype(o_ref.dtype)
        x_cat = jnp.concatenate([x_s[j, r0:r0 + tc, :].astype(BF16) for j in range(2 * ns)], axis=1)
        o_ref[b, :, D_LRU:U_COLS] = _s5_out(x_cat, u_ref[b, :, D_LRU:U_COLS], cbd_ref, d_ref,
                                            wglu_ref, bglu_ref).astype(o_ref.dtype)
        cbuf_ref[b] = ext[b, 8 - (CONV_W - 1):8, :]


def _rec_prompt(u, pr, tc):
    nb, s, _ = u.shape
    pitch = tc + 8
    nl = D_LRU // LANES
    ns = S5_P // LANES
    full = lambda a: pl.BlockSpec(a.shape, lambda c, _n=a.ndim: (0,) * _n)
    params = [pr["cw"], pr["cb"], pr["wg"], pr["bg"], pr["c8"], pr["lbr"], pr["lbi"],
              pr["bbd"], pr["cbd"], pr["d"], pr["wglu"], pr["bglu"]]
    st = lambda *shape: pl.BlockSpec(shape, lambda c, _n=len(shape): (0,) * _n)
    return pl.pallas_call(
        functools.partial(_rec_prompt_kernel, nb=nb, tc=tc, pitch=pitch),
        grid=(s // tc,),
        in_specs=[pl.BlockSpec((nb, tc, U_COLS), lambda c: (0, c, 0))] + [full(a) for a in params],
        out_specs=[pl.BlockSpec((nb, tc, U_COLS), lambda c: (0, c, 0)),
                   st(nb, D_LRU), st(nb, CONV_W - 1, D_LRU), st(nb, S5_P), st(nb, S5_P)],
        out_shape=[jax.ShapeDtypeStruct((nb, s, U_COLS), BF16),
                   jax.ShapeDtypeStruct((nb, D_LRU), F32),
                   jax.ShapeDtypeStruct((nb, CONV_W - 1, D_LRU), F32),
                   jax.ShapeDtypeStruct((nb, S5_P), F32),
                   jax.ShapeDtypeStruct((nb, S5_P), F32)],
        scratch_shapes=[pltpu.VMEM((nb, tc + 8, D_LRU), F32),
                        pltpu.VMEM((nl, nb * pitch, LANES), F32),
                        pltpu.VMEM((nl, nb * pitch, LANES), F32),
                        pltpu.VMEM((nl, nb * pitch, LANES), F32),
                        pltpu.VMEM((2 * ns, nb * pitch, LANES), F32),
                        pltpu.VMEM((2 * ns, nb * pitch, LANES), F32),
                        pltpu.VMEM((nl, nb, LANES), F32),
                        pltpu.VMEM((2 * ns, nb, LANES), F32)],
        compiler_params=_cparams(("arbitrary",)),
        name="rec_prompt",
    )(u, *params)


def _rec_sample_kernel(u_ref, h0_ref, buf0_ref, s0r_ref, s0i_ref,
                       cw_ref, cb_ref, wg_ref, bg_ref, c8_ref, lbr_ref, lbi_ref,
                       bbd_ref, cbd_ref, d_ref, wglu_ref, bglu_ref,
                       o_ref, hl_ref, cbuf_ref, sre_ref, sim_ref, *, nt):
    ns = S5_P // LANES
    ext = [buf0_ref[j] for j in range(CONV_W - 1)] + [u_ref[t, :, 0:D_LRU] for t in range(nt)]
    xcs = []
    for t in range(nt):
        xc = cb_ref[...]
        for j in range(CONV_W):
            xc = xc + ext[t + j] * cw_ref[j:j + 1, :]
        xcs.append(xc)
    a, bb = _lru_gates(jnp.concatenate(xcs, axis=0), wg_ref, bg_ref, c8_ref)
    nbat = h0_ref.shape[0]
    h = h0_ref[...]
    for t in range(nt):
        h = a[t * nbat:(t + 1) * nbat] * h + bb[t * nbat:(t + 1) * nbat]
        o_ref[t, :, 0:D_LRU] = h.astype(o_ref.dtype)
    hl_ref[...] = h
    for j in range(CONV_W - 1):
        cbuf_ref[j] = ext[nt + j]

    sx = jnp.concatenate([u_ref[t, :, D_LRU:U_COLS] for t in range(nt)], axis=0)
    bu = jnp.dot(sx.astype(BF16), bbd_ref[...], preferred_element_type=F32)
    lr = jnp.concatenate([lbr_ref[j] for j in range(ns)], axis=1)
    li = jnp.concatenate([lbi_ref[j] for j in range(ns)], axis=1)
    xr = s0r_ref[...]
    xi = s0i_ref[...]
    xs = []
    for t in range(nt):
        but = bu[t * nbat:(t + 1) * nbat]
        bur = jnp.concatenate([but[:, 2 * j * LANES:(2 * j + 1) * LANES] for j in range(ns)], axis=1)
        bui = jnp.concatenate([but[:, (2 * j + 1) * LANES:(2 * j + 2) * LANES] for j in range(ns)], axis=1)
        xr, xi = lr * xr - li * xi + bur, lr * xi + li * xr + bui
        xs.append(jnp.concatenate([xr, xi], axis=1).astype(BF16))
    sre_ref[...] = xr
    sim_ref[...] = xi
    s_out = _s5_out(jnp.concatenate(xs, axis=0), sx, cbd_ref, d_ref, wglu_ref, bglu_ref)
    for t in range(nt):
        o_ref[t, :, D_LRU:U_COLS] = s_out[t * nbat:(t + 1) * nbat].astype(o_ref.dtype)


def _rec_sample(u_t, h0, buf0_t, s0r, s0i, pr):
    nt, nbat, _ = u_t.shape
    params = [pr["cw"], pr["cb"], pr["wg"], pr["bg"], pr["c8"], pr["lbr"], pr["lbi"],
              pr["bbd"], pr["cbd"], pr["d"], pr["wglu"], pr["bglu"]]
    return pl.pallas_call(
        functools.partial(_rec_sample_kernel, nt=nt),
        out_shape=[jax.ShapeDtypeStruct((nt, nbat, U_COLS), BF16),
                   jax.ShapeDtypeStruct((nbat, D_LRU), F32),
                   jax.ShapeDtypeStruct((CONV_W - 1, nbat, D_LRU), F32),
                   jax.ShapeDtypeStruct((nbat, S5_P), F32),
                   jax.ShapeDtypeStruct((nbat, S5_P), F32)],
        compiler_params=pltpu.CompilerParams(vmem_limit_bytes=VMEM_LIMIT),
        name="rec_sample",
    )(u_t, h0, buf0_t, s0r, s0i, *params)


def _rec_params(lru_conv_w, lru_conv_b, lru_w_r, lru_b_r, lru_w_i, lru_b_i, lru_lam,
                s5_a_re, s5_a_im, s5_log_dt, s5_b_re, s5_b_im, s5_c_re, s5_c_im, s5_d, s5_w_glu, s5_b_glu):
    ns = S5_P // LANES
    bd = lambda w: jax.scipy.linalg.block_diag(*[w[n] for n in range(w.shape[0])])
    wg = jnp.concatenate([bd(lru_w_r), bd(lru_w_i)], axis=1).astype(BF16)
    bg = jnp.concatenate([lru_b_r, lru_b_i]).reshape(1, 2 * D_LRU)
    c8 = (LRU_C * jax.nn.log_sigmoid(lru_lam.astype(F32))).reshape(1, D_LRU)
    dt = jnp.exp(s5_log_dt.astype(F32))[:, None]
    ar = s5_a_re.astype(F32)
    ai = s5_a_im.astype(F32)
    mag = jnp.exp(ar * dt)
    lb_re = mag * jnp.cos(ai * dt)
    lb_im = mag * jnp.sin(ai * dt)
    den = ar * ar + ai * ai
    n_re = lb_re - 1.0
    co_re = (n_re * ar + lb_im * ai) / den
    co_im = (lb_im * ar - n_re * ai) / den
    bb_re = co_re[..., None] * s5_b_re - co_im[..., None] * s5_b_im
    bb_im = co_re[..., None] * s5_b_im + co_im[..., None] * s5_b_re
    bre = bd(jnp.swapaxes(bb_re, 1, 2))
    bim = bd(jnp.swapaxes(bb_im, 1, 2))
    bbd = jnp.stack([bre.reshape(D_S5, ns, LANES), bim.reshape(D_S5, ns, LANES)], axis=2)
    bbd = bbd.reshape(D_S5, 2 * S5_P).astype(BF16)
    cre = bd(jnp.swapaxes(s5_c_re, 1, 2))
    cim = bd(jnp.swapaxes(s5_c_im, 1, 2))
    cbd = jnp.concatenate([cre, -cim], axis=0).astype(BF16)
    return dict(cw=lru_conv_w, cb=lru_conv_b.reshape(1, D_LRU), wg=wg, bg=bg, c8=c8,
                lbr=lb_re.reshape(ns, 1, LANES), lbi=lb_im.reshape(ns, 1, LANES),
                bbd=bbd, cbd=cbd, d=s5_d.reshape(1, D_S5),
                wglu=s5_w_glu.astype(BF16), bglu=s5_b_glu.reshape(1, D_S5))


def _moba_prompt_kernel(pt_ref, qt_ref, krm_ref, kt_ref, vt_ref, cache_ref, o_ref, ks_ref,
                        km_s, vb_s, bias_s, sc_s, pbuf, psem, *, s, nblk, grp, nh, layer, pg, nsteps):
    qi = pl.program_id(2)
    step = (pl.program_id(0) * (N_HEADS // nh) + pl.program_id(1)) * nblk + qi
    tq = MOBA_BLOCK
    hd = HEAD_DIM
    gk = grp * tq

    @pl.when(qi == 0)
    def _():
        r = lax.broadcasted_iota(jnp.int32, (nblk, s), 1) // MOBA_BLOCK
        n = lax.broadcasted_iota(jnp.int32, (nblk, s), 0)
        pm = jnp.where(r == n, 1.0 / MOBA_BLOCK, 0.0).astype(F32)
        for hh in range(nh):
            km_s[hh] = lax.dot_general(pm, kt_ref[0, hh * hd:(hh + 1) * hd, :], NT_DIMS,
                                       precision=HIGHEST, preferred_element_type=F32)
        ones_row = (lax.broadcasted_iota(jnp.int32, (VB_ROWS - hd, gk), 0) == 0).astype(BF16)
        for g in range(nblk // grp):
            for hh in range(nh):
                vb_s[g, hh, 0:hd, :] = vt_ref[0, hh * hd:(hh + 1) * hd, g * gk:(g + 1) * gk].astype(BF16)
                vb_s[g, hh, hd:VB_ROWS, :] = ones_row

    qt = qt_ref[0]
    qs = (qt * (hd ** -0.5 * LOG2E)).astype(BF16)
    feat = lax.broadcasted_iota(jnp.int32, (nh * hd, tq), 0)
    blk_iota = lax.broadcasted_iota(jnp.int32, (nblk, tq), 0)
    past = blk_iota < qi
    ws = []
    for hh in range(nh):
        ws.append(jnp.where((feat >= hh * hd) & (feat < (hh + 1) * hd), qs, jnp.zeros_like(qs)))
        sb = jnp.dot(km_s[hh], qt[hh * hd:(hh + 1) * hd, :], precision=HIGHEST,
                     preferred_element_type=F32)
        sb = jnp.where(past, sb, -jnp.inf)
        cnt = jnp.zeros((nblk, tq), jnp.int32)
        for i in range(nblk):
            ri = sb[i:i + 1, :]
            beats = (ri > sb) | ((ri == sb) & (i < blk_iota))
            cnt = cnt + beats.astype(jnp.int32)
        sel = (cnt < MOBA_TOPK) & past
        bias_s[hh] = jnp.where(sel | (blk_iota == qi), 0.0, NEG).astype(F32)

    def scores(g, hh):
        st = pl.multiple_of(g * gk, gk)
        sc = jnp.dot(krm_ref[0, pl.ds(st, gk), :], ws[hh], preferred_element_type=F32)
        return jnp.concatenate(
            [sc[i * tq:(i + 1) * tq] + bias_s[hh, pl.ds(g * grp + i, 1), :] for i in range(grp)], axis=0)

    def pass1(g, ms):
        st = pl.multiple_of(g * gk, gk)
        out = []
        for hh in range(nh):
            sc = scores(g, hh)
            sc_s[hh, pl.ds(st, gk), :] = sc
            out.append(jnp.maximum(ms[hh], jnp.max(sc, axis=0, keepdims=True)))
        return tuple(out)

    g_own = qi // grp
    ms = lax.fori_loop(0, g_own, pass1, tuple(jnp.full((1, tq), NEG, F32) for _ in range(nh)))

    _ksum_step(pt_ref, cache_ref, ks_ref, pbuf, psem, step, layer=layer, nsteps=nsteps, pg=pg)

    r_own = qi % grp
    st_own = pl.multiple_of(g_own * gk, gk)
    causal = (lax.broadcasted_iota(jnp.int32, (tq, tq), 0) <= lax.broadcasted_iota(jnp.int32, (tq, tq), 1))

    def own_pass1(nb_own):
        rows = nb_own * tq

        def run(ms):
            out = []
            for hh in range(nh):
                sc = jnp.dot(krm_ref[0, pl.ds(st_own, rows), :], ws[hh], preferred_element_type=F32)
                parts = [sc[i * tq:(i + 1) * tq] + bias_s[hh, pl.ds(g_own * grp + i, 1), :]
                         for i in range(nb_own - 1)]
                parts.append(jnp.where(causal, sc[rows - tq:rows], NEG))
                sc = jnp.concatenate(parts, axis=0)
                sc_s[hh, pl.ds(st_own, rows), :] = sc
                out.append(jnp.maximum(ms[hh], jnp.max(sc, axis=0, keepdims=True)))
            return tuple(out)
        return run

    ms = lax.switch(r_own, [own_pass1(i + 1) for i in range(grp)], ms)

    def pass2_group(g, rows, accs):
        st = pl.multiple_of(g * gk, gk)
        out = []
        for hh in range(nh):
            p = jnp.exp2(sc_s[hh, pl.ds(st, rows), :] - ms[hh])
            out.append(accs[hh] + jnp.dot(vb_s[g, hh, :, 0:rows], p.astype(BF16), preferred_element_type=F32))
        return tuple(out)

    accs = lax.fori_loop(0, g_own, lambda g, a: pass2_group(g, gk, a),
                         tuple(jnp.zeros((VB_ROWS, tq), F32) for _ in range(nh)))
    accs = lax.switch(r_own, [functools.partial(pass2_group, g_own, (i + 1) * tq) for i in range(grp)], accs)
    out_t = jnp.concatenate([acc[0:hd] / acc[hd:hd + 1] for acc in accs], axis=0)
    o_ref[0] = jnp.transpose(out_t).astype(o_ref.dtype)


def _moba_prompt(qt, krm, kt, vt, pt_flat, cache_kt, layer):
    nb, _, s = qt.shape
    nblk = s // MOBA_BLOCK
    tq = MOBA_BLOCK
    nh = 4
    grp = 4
    hp = nh * HEAD_DIM
    nhp = N_HEADS // nh
    nsteps = nb * nhp * nblk
    pg = pt_flat.shape[0] // nsteps
    nbat = pt_flat.shape[0] // N_PAGES
    steps_per_b = N_PAGES // pg
    once = dict(pipeline_mode=pl.Buffered(1))
    return pl.pallas_call(
        functools.partial(_moba_prompt_kernel, s=s, nblk=nblk, grp=grp, nh=nh, layer=layer, pg=pg,
                          nsteps=nsteps),
        grid_spec=pltpu.PrefetchScalarGridSpec(
            num_scalar_prefetch=1,
            grid=(nb, nhp, nblk),
            in_specs=[pl.BlockSpec((1, hp, tq), lambda b, h, i, pt: (b, h, i)),
                      pl.BlockSpec((1, s, hp), lambda b, h, i, pt: (b, 0, h), **once),
                      pl.BlockSpec((1, hp, s), lambda b, h, i, pt: (b, h, 0), **once),
                      pl.BlockSpec((1, hp, s), lambda b, h, i, pt: (b, h, 0), **once),
                      pl.BlockSpec(memory_space=pl.ANY)],
            out_specs=[pl.BlockSpec((1, tq, hp), lambda b, h, i, pt: (b, i, h)),
                       pl.BlockSpec((1, N_HEADS, HEAD_DIM, LANES),
                                    lambda b, h, i, pt: (((b * nhp + h) * nblk + i) // steps_per_b, 0, 0, 0))],
            scratch_shapes=[pltpu.VMEM((nh, nblk, HEAD_DIM), F32),
                            pltpu.VMEM((nblk // grp, nh, VB_ROWS, grp * tq), BF16),
                            pltpu.VMEM((nh, nblk, tq), F32),
                            pltpu.VMEM((nh, s, tq), F32),
                            pltpu.VMEM((2, pg, N_HEADS, HEAD_DIM, PAGE_SIZE), F32),
                            pltpu.SemaphoreType.DMA((2, pg))]),
        out_shape=[jax.ShapeDtypeStruct((nb, s, D_ATT), BF16),
                   jax.ShapeDtypeStruct((nbat, N_HEADS, HEAD_DIM, LANES), F32)],
        compiler_params=_cparams(("arbitrary", "arbitrary", "arbitrary")),
        name="moba_prompt",
    )(pt_flat, qt, krm, kt, vt, cache_kt)


PAGES_PER_BLOCK = MOBA_BLOCK // PAGE_SIZE
N_PAGES = PAST_LEN // PAGE_SIZE
N_FULL = PAST_LEN // MOBA_BLOCK
SAMPLE_SLOTS = 3


def _ksum_step(pt_ref, cache_ref, o_ref, buf, sem, n, *, layer, nsteps, pg):
    slot = n % 2
    steps_per_b = N_PAGES // pg
    g = n % steps_per_b
    bpg = pg // PAGES_PER_BLOCK

    def start(step, sl):
        for j in range(pg):
            pid = pt_ref[step * pg + j]
            pltpu.make_async_copy(cache_ref.at[layer, pid], buf.at[sl, j], sem.at[sl, j]).start()

    @pl.when(n == 0)
    def _():
        start(0, 0)

    @pl.when(n + 1 < nsteps)
    def _():
        start(n + 1, 1 - slot)

    @pl.when(g == 0)
    def _():
        o_ref[...] = jnp.zeros(o_ref.shape, F32)

    for j in range(pg):
        pltpu.make_async_copy(buf.at[slot, j], buf.at[slot, j], sem.at[slot, j]).wait()
    lane = lax.broadcasted_iota(jnp.int32, (HEAD_DIM, LANES), 1)
    for hh in range(N_HEADS):
        acc = o_ref[0, hh]
        for blk in range(bpg):
            tile = buf[slot, PAGES_PER_BLOCK * blk, hh]
            for pp in range(1, PAGES_PER_BLOCK):
                tile = tile + buf[slot, PAGES_PER_BLOCK * blk + pp, hh]
            acc = jnp.where(lane == g * bpg + blk, jnp.sum(tile, axis=1, keepdims=True), acc)
        o_ref[0, hh] = acc


def _topk_kernel(q_ref, ks_ref, o_ref):
    nq = q_ref.shape[2]
    lane = lax.broadcasted_iota(jnp.int32, (nq, LANES), 1)
    for hh in range(N_HEADS):
        km = ks_ref[0, hh] * (1.0 / MOBA_BLOCK)
        sb = jnp.dot(q_ref[0, hh], km, precision=HIGHEST, preferred_element_type=F32)
        sb = jnp.where(lane < N_FULL, sb, -jnp.inf)
        out = jnp.zeros((nq, LANES), jnp.int32)
        for r in range(MOBA_TOPK):
            mx = jnp.max(sb, axis=1, keepdims=True)
            idx = jnp.min(jnp.where(sb == mx, lane, LANES), axis=1, keepdims=True)
            out = jnp.where(lane == r, idx, out)
            sb = jnp.where(lane == idx, -jnp.inf, sb)
        o_ref[0, hh] = out


def _sample_topk(q_pad, ksum_t):
    nbat, nh, nq, hd = q_pad.shape
    return pl.pallas_call(
        _topk_kernel,
        grid=(nbat,),
        in_specs=[pl.BlockSpec((1, nh, nq, hd), lambda b: (b, 0, 0, 0)),
                  pl.BlockSpec((1, nh, hd, LANES), lambda b: (b, 0, 0, 0))],
        out_specs=pl.BlockSpec((1, nh, nq, LANES), lambda b: (b, 0, 0, 0)),
        out_shape=jax.ShapeDtypeStruct((nbat, nh, nq, LANES), jnp.int32),
        compiler_params=_cparams(("arbitrary",)),
        name="sample_topk",
    )(q_pad, ksum_t)


def _slab_copy(src, buf, sem, slot, j):
    return pltpu.make_async_copy(src, buf.at[slot, j], sem.at[slot, j])


def _moba_sample_kernel(idx_ref, pt_ref, q_ref, kn_ref, vn_ref, ck_ref, cv_ref, o_ref,
                        kbuf, vbuf, ksem, vsem, *, layer, nsteps, nt, hps):
    n = pl.program_id(0)
    slot = n % SAMPLE_SLOTS
    b = n // (N_HEADS // hps)
    nsel = MOBA_TOPK * PAGES_PER_BLOCK
    nslab = nt * nsel
    ahead = SAMPLE_SLOTS - 1

    def start(step, sl):
        sb = step // (N_HEADS // hps)
        for e in range(hps):
            bh = step * hps + e
            sh = bh % N_HEADS
            for t in range(nt):
                for r in range(MOBA_TOPK):
                    blk = idx_ref[(bh * nt + t) * MOBA_TOPK + r]
                    for pp in range(PAGES_PER_BLOCK):
                        pid = pt_ref[sb * N_PAGES + blk * PAGES_PER_BLOCK + pp]
                        j = e * nslab + (t * MOBA_TOPK + r) * PAGES_PER_BLOCK + pp
                        _slab_copy(ck_ref.at[layer, pid, sh], kbuf, ksem, sl, j).start()
                        _slab_copy(cv_ref.at[layer, pid, sh], vbuf, vsem, sl, j).start()

    def wait(sl):
        for j in range(hps * nslab):
            _slab_copy(kbuf.at[sl, j], kbuf, ksem, sl, j).wait()
            _slab_copy(vbuf.at[sl, j], vbuf, vsem, sl, j).wait()

    @pl.when(n == 0)
    def _():
        for s0 in range(ahead):
            start(s0, s0)

    @pl.when(n + ahead < nsteps)
    def _():
        start(n + ahead, (n + ahead) % SAMPLE_SLOTS)

    wait(slot)

    nq = q_ref.shape[2]
    ntok = kn_ref.shape[2]
    nkeys = nsel * PAGE_SIZE
    qrow = lax.broadcasted_iota(jnp.int32, (nq, nkeys), 0)
    orow = lax.broadcasted_iota(jnp.int32, (nq, ntok), 0)
    opos = lax.broadcasted_iota(jnp.int32, (nq, ntok), 1) - b * nt
    own_ok = (opos >= 0) & (opos <= orow)
    for e in range(hps):
        qs = (q_ref[0, e] * (HEAD_DIM ** -0.5)).astype(BF16)
        base = e * nslab
        scores = []
        for t in range(nt):
            kt = jnp.concatenate([kbuf[slot, base + t * nsel + i] for i in range(nsel)], axis=1).astype(BF16)
            st = jnp.dot(qs, kt, preferred_element_type=F32)
            scores.append(jnp.where(qrow == t, st, -jnp.inf))
        so = jnp.dot(qs, kn_ref[e].astype(BF16), preferred_element_type=F32)
        so = jnp.where(own_ok, so, -jnp.inf)
        m = jnp.max(so, axis=1, keepdims=True)
        for st in scores:
            m = jnp.maximum(m, jnp.max(st, axis=1, keepdims=True))
        po = jnp.exp(so - m)
        den = jnp.sum(po, axis=1, keepdims=True)
        acc = lax.dot_general(po.astype(BF16), vn_ref[e].astype(BF16), NT_DIMS, preferred_element_type=F32)
        for t in range(nt):
            pt_ = jnp.exp(scores[t] - m)
            den = den + jnp.sum(pt_, axis=1, keepdims=True)
            vt = jnp.concatenate([vbuf[slot, base + t * nsel + i] for i in range(nsel)], axis=1).astype(BF16)
            acc = acc + lax.dot_general(pt_.astype(BF16), vt, NT_DIMS, preferred_element_type=F32)
        o_ref[0, e] = acc / den


def _moba_sample(q_pad, kt_new, vt_new, cache_kt, cache_vt, idx_flat, pt_flat, layer, nt):
    nbat, nh, nq, hd = q_pad.shape
    ntok = kt_new.shape[1]
    hps = 2
    npair = nh // hps
    nsteps = nbat * npair
    nslab = hps * nt * MOBA_TOPK * PAGES_PER_BLOCK
    qtok = pl.BlockSpec((1, hps, nq, hd), lambda n, idx, pt: (n // npair, n % npair, 0, 0))
    new = pl.BlockSpec((hps, hd, ntok), lambda n, idx, pt: (n % npair, 0, 0))
    return pl.pallas_call(
        functools.partial(_moba_sample_kernel, layer=layer, nsteps=nsteps, nt=nt, hps=hps),
        grid_spec=pltpu.PrefetchScalarGridSpec(
            num_scalar_prefetch=2,
            grid=(nsteps,),
            in_specs=[qtok, new, new, pl.BlockSpec(memory_space=pl.ANY), pl.BlockSpec(memory_space=pl.ANY)],
            out_specs=qtok,
            scratch_shapes=[pltpu.VMEM((SAMPLE_SLOTS, nslab, hd, PAGE_SIZE), F32),
                            pltpu.VMEM((SAMPLE_SLOTS, nslab, hd, PAGE_SIZE), F32),
                            pltpu.SemaphoreType.DMA((SAMPLE_SLOTS, nslab)),
                            pltpu.SemaphoreType.DMA((SAMPLE_SLOTS, nslab))]),
        out_shape=jax.ShapeDtypeStruct((nbat, nh, nq, hd), F32),
        compiler_params=_cparams(("arbitrary",)),
        name="moba_sample",
    )(idx_flat, pt_flat, q_pad, kt_new.reshape(nh, hd, ntok), vt_new.reshape(nh, hd, ntok),
      cache_kt, cache_vt)


def _rope_tables(pos):
    inv = jnp.power(ROPE_THETA, -jnp.arange(HALF, dtype=F32) / HALF)
    ang = pos.astype(F32)[:, None] * inv[None, :]
    cos, sin = jnp.cos(ang), jnp.sin(ang)
    reps = LANES // HEAD_DIM
    return (jnp.tile(jnp.concatenate([cos, cos], axis=1), (1, reps)),
            jnp.tile(jnp.concatenate([-sin, sin], axis=1), (1, reps)),
            cos.T, sin.T)


def _split_w_in(w):
    a_x, a_g, q, k, v, b_g, s_x, s_g = jnp.split(
        w, (D_LRU, 2 * D_LRU, 2 * D_LRU + D_ATT, 2 * D_LRU + 2 * D_ATT, 2 * D_LRU + 3 * D_ATT,
            2 * D_LRU + 4 * D_ATT, 2 * D_LRU + 4 * D_ATT + D_S5), axis=1)
    w_row = jnp.concatenate([a_x, s_x, a_g, b_g, s_g, k], axis=1).astype(BF16)
    w_t = jnp.concatenate([q, k, v], axis=1).T.astype(BF16)
    return w_row, w_t


def kernel(x_prompt, x_sample, cache_k, cache_v, state_lru_h, state_lru_conv, state_s5_re, state_s5_im,
           page_table, c_prompt, c_sample, norm_g, w_ada, b_ada, w_in, w_out, lru_conv_w, lru_conv_b,
           lru_w_r, lru_b_r, lru_w_i, lru_b_i, lru_lam, s5_a_re, s5_a_im, s5_log_dt, s5_b_re, s5_b_im,
           s5_c_re, s5_c_im, s5_d, s5_w_glu, s5_b_glu, final_g):
    nbp, seq, _ = x_prompt.shape
    nbs, nts, _ = x_sample.shape
    ntok_s = nbs * nts

    rope_p = _rope_tables(jnp.arange(seq, dtype=jnp.int32))
    rope_s = _rope_tables(PAST_LEN + (jnp.arange(ntok_s, dtype=jnp.int32) % nts))

    n_c = nbp + nbs
    c_all = jnp.concatenate([c_prompt, c_sample, jnp.zeros((-n_c % 8, D_MODEL), F32)], axis=0)
    mod = _modulation(c_all, w_ada, b_ada)

    cache_kt = jnp.swapaxes(cache_k, 3, 4)
    cache_vt = jnp.swapaxes(cache_v, 3, 4)
    pt_flat = page_table.reshape(-1)
    xp = x_prompt
    xs = x_sample.reshape(1, ntok_s, D_MODEL)
    outs_p = [[] for _ in range(6)]
    outs_s = [[] for _ in range(6)]
    for l in range(DEPTH):
        last = l == DEPTH - 1
        w_row, w_t = _split_w_in(w_in[l])
        w_o = w_out[l].astype(BF16)
        pr = _rec_params(lru_conv_w[l], lru_conv_b[l], lru_w_r[l], lru_b_r[l], lru_w_i[l], lru_b_i[l],
                         lru_lam[l], s5_a_re[l], s5_a_im[l], s5_log_dt[l], s5_b_re[l], s5_b_im[l],
                         s5_c_re[l], s5_c_im[l], s5_d[l], s5_w_glu[l], s5_b_glu[l])
        shift, scale, gate = jnp.split(mod[l], 3, axis=-1)

        mp = lambda a: a[:nbp].reshape(nbp, 1, D_MODEL)
        u, gates, krm, qt, kt, vt = _inproj(xp, mp(scale), mp(shift), norm_g[l], w_row, w_t, rope_p, tm=512)
        as_out, h_last, cbuf, s_re, s_im = _rec_prompt(u, pr, tc=256)
        att, ksum_t = _moba_prompt(qt, krm, kt, vt, pt_flat, cache_kt, l)
        xp = _outproj(xp, as_out, att, gates, mp(gate), w_o, final_g, tm=512, final=last)
        for lst, val in zip(outs_p, (kt.reshape(nbp, N_HEADS, HEAD_DIM, seq), vt.reshape(nbp, N_HEADS, HEAD_DIM, seq),
                                     h_last, cbuf, s_re.reshape(nbp, S5_GROUPS, S5_STATE),
                                     s_im.reshape(nbp, S5_GROUPS, S5_STATE))):
            lst.append(val)

        ms = lambda a: jnp.repeat(a[nbp:n_c], nts, axis=0).reshape(1, ntok_s, D_MODEL)
        u, gates, _, qt, kt, vt = _inproj(xs, ms(scale), ms(shift), norm_g[l], w_row, w_t, rope_s, tm=ntok_s)
        to_heads = lambda a: a.reshape(N_HEADS, HEAD_DIM, nbs, nts).transpose(2, 0, 3, 1)
        q, k, v = to_heads(qt), to_heads(kt), to_heads(vt)
        u_t = u.reshape(nbs, nts, U_COLS).transpose(1, 0, 2)
        as_t, h_last, cbuf_t, s_re, s_im = _rec_sample(
            u_t, state_lru_h[l], state_lru_conv[l].transpose(1, 0, 2),
            state_s5_re[l].reshape(nbs, S5_P), state_s5_im[l].reshape(nbs, S5_P), pr)
        as_out = as_t.transpose(1, 0, 2).reshape(1, ntok_s, U_COLS)
        q_pad = jnp.pad(q, ((0, 0), (0, 0), (0, -nts % 8), (0, 0)))
        idx = _sample_topk(q_pad, ksum_t)[:, :, :nts, :MOBA_TOPK].reshape(-1)
        att = _moba_sample(q_pad, kt[0], vt[0], cache_kt, cache_vt, idx, pt_flat, l, nts)[:, :, :nts]
        att = att.transpose(0, 2, 1, 3).reshape(1, ntok_s, D_ATT).astype(BF16)
        xs = _outproj(xs, as_out, att, gates, ms(gate), w_o, final_g, tm=ntok_s, final=last)
        for lst, val in zip(outs_s, (k, v, h_last, cbuf_t.transpose(1, 0, 2),
                                     s_re.reshape(nbs, S5_GROUPS, S5_STATE),
                                     s_im.reshape(nbs, S5_GROUPS, S5_STATE))):
            lst.append(val)

    k_p, v_p = (jnp.swapaxes(jnp.stack(o), 3, 4) for o in outs_p[:2])
    return (xp, xs.reshape(nbs, nts, D_MODEL), k_p, v_p,
            *[jnp.stack(o) for o in outs_p[2:]], *[jnp.stack(o) for o in outs_s])
```

```python
import functools

import jax
import jax.numpy as jnp
from jax import lax
from jax.experimental import pallas as pl
from jax.experimental.pallas import tpu as pltpu

F32 = jnp.float32
BF16 = jnp.bfloat16
HIGHEST = lax.Precision.HIGHEST

D_MODEL = 1024
DEPTH = 2
PAST_LEN = 16384
PAGE_SIZE = 128
D_LRU = 256
LRU_BLOCKS = 4
LRU_BW = D_LRU // LRU_BLOCKS
CONV_W = 4
LRU_C = 8.0
N_HEADS = 8
HEAD_DIM = 64
HALF = HEAD_DIM // 2
D_ATT = N_HEADS * HEAD_DIM
MOBA_BLOCK = 256
MOBA_TOPK = 3
ROPE_THETA = 10000.0
D_S5 = 256
S5_GROUP = 16
S5_GROUPS = D_S5 // S5_GROUP
S5_STATE = 64
S5_P = S5_GROUPS * S5_STATE
D_MIX = D_LRU + D_ATT + D_S5
EPS = 1e-6

LANES = 128
BF16_ROWS = 16
NEG = -1e30
LOG2E = 1.4426950408889634
VB_ROWS = HEAD_DIM + BF16_ROWS

U_COLS = D_LRU + D_S5
G_COLS = D_MIX
OFF_U = 0
OFF_G = OFF_U + U_COLS
OFF_KR = OFF_G + G_COLS

VMEM_LIMIT = 56 * 1024 * 1024

NT_DIMS = (((1,), (1,)), ((), ()))


def _cparams(sem):
    return pltpu.CompilerParams(dimension_semantics=sem, vmem_limit_bytes=VMEM_LIMIT)


def _silu(x):
    return x * jax.nn.sigmoid(x)


def _mod_kernel(c_ref, w_ref, b_ref, o_ref):
    o_ref[0] = jnp.dot(_silu(c_ref[...]), w_ref[0], precision=HIGHEST,
                       preferred_element_type=F32) + b_ref[0]


def _modulation(c_all, w_ada, b_ada):
    n = c_all.shape[0]
    tn = D_MODEL
    return pl.pallas_call(
        _mod_kernel,
        grid=(DEPTH, 3 * D_MODEL // tn),
        in_specs=[pl.BlockSpec((n, D_MODEL), lambda l, j: (0, 0)),
                  pl.BlockSpec((1, D_MODEL, tn), lambda l, j: (l, 0, j)),
                  pl.BlockSpec((1, 1, tn), lambda l, j: (l, 0, j))],
        out_specs=pl.BlockSpec((1, n, tn), lambda l, j: (l, 0, j)),
        out_shape=jax.ShapeDtypeStruct((DEPTH, n, 3 * D_MODEL), F32),
        compiler_params=_cparams(("arbitrary", "arbitrary")),
        name="modulation",
    )(c_all, w_ada, b_ada.reshape(DEPTH, 1, 3 * D_MODEL))


def _inproj_kernel(x_ref, sc_ref, sh_ref, g_ref, w_ref, wt_ref, cos_ref, sin_ref, cost_ref, sint_ref,
                   u_ref, gate_ref, krm_ref, qt_ref, kt_ref, vt_ref):
    x = x_ref[0]
    h = x * lax.rsqrt(jnp.mean(x * x, axis=-1, keepdims=True) + EPS) * g_ref[...]
    h = h * (1.0 + sc_ref[0]) + sh_ref[0]
    hb = h.astype(BF16)

    def proj(lo, n):
        return jnp.dot(hb, w_ref[:, lo:lo + n], preferred_element_type=F32)

    u_ref[0] = proj(OFF_U, U_COLS)
    gate_ref[0] = _silu(proj(OFF_G, G_COLS)).astype(gate_ref.dtype)

    cos = cos_ref[...]
    sin = sin_ref[...]
    lane = lax.broadcasted_iota(jnp.int32, cos.shape, 1)
    first_half = (lane & (HEAD_DIM - 1)) < HALF
    p = proj(OFF_KR, D_ATT)
    for j in range(D_ATT // LANES):
        xs = p[:, j * LANES:(j + 1) * LANES]
        partner = jnp.where(first_half, pltpu.roll(xs, LANES - HALF, 1), pltpu.roll(xs, HALF, 1))
        krm_ref[0, :, j * LANES:(j + 1) * LANES] = (xs * cos + partner * sin).astype(BF16)

    cost = cost_ref[...]
    sint = sint_ref[...]
    for n, (ref, use_rope) in enumerate(((qt_ref, True), (kt_ref, True), (vt_ref, False))):
        pt = lax.dot_general(wt_ref[n * D_ATT:(n + 1) * D_ATT, :], hb, NT_DIMS,
                             preferred_element_type=F32)
        if not use_rope:
            ref[0] = pt
            continue
        for hh in range(N_HEADS):
            r0 = hh * HEAD_DIM
            x1 = pt[r0:r0 + HALF]
            x2 = pt[r0 + HALF:r0 + HEAD_DIM]
            ref[0, r0:r0 + HALF, :] = x1 * cost - x2 * sint
            ref[0, r0 + HALF:r0 + HEAD_DIM, :] = x2 * cost + x1 * sint


def _inproj(x, scale, shift, g, w_row, w_t, rope, tm):
    nb, s, _ = x.shape
    cos_r, sin_r, cos_t, sin_t = rope
    rm = scale.shape[1]
    mod_block = (1, tm, D_MODEL) if rm == s else (1, 1, D_MODEL)
    mod_map = (lambda b, i: (b, i, 0)) if rm == s else (lambda b, i: (b, 0, 0))
    fm_spec = pl.BlockSpec((1, D_ATT, tm), lambda b, i: (b, 0, i))
    fm_shape = jax.ShapeDtypeStruct((nb, D_ATT, s), F32)
    return pl.pallas_call(
        _inproj_kernel,
        grid=(nb, s // tm),
        in_specs=[pl.BlockSpec((1, tm, D_MODEL), lambda b, i: (b, i, 0)),
                  pl.BlockSpec(mod_block, mod_map),
                  pl.BlockSpec(mod_block, mod_map),
                  pl.BlockSpec((1, D_MODEL), lambda b, i: (0, 0)),
                  pl.BlockSpec(w_row.shape, lambda b, i: (0, 0)),
                  pl.BlockSpec(w_t.shape, lambda b, i: (0, 0)),
                  pl.BlockSpec((tm, LANES), lambda b, i: (i, 0)),
                  pl.BlockSpec((tm, LANES), lambda b, i: (i, 0)),
                  pl.BlockSpec((HALF, tm), lambda b, i: (0, i)),
                  pl.BlockSpec((HALF, tm), lambda b, i: (0, i))],
        out_specs=[pl.BlockSpec((1, tm, U_COLS), lambda b, i: (b, i, 0)),
                   pl.BlockSpec((1, tm, G_COLS), lambda b, i: (b, i, 0)),
                   pl.BlockSpec((1, tm, D_ATT), lambda b, i: (b, i, 0)),
                   fm_spec, fm_spec, fm_spec],
        out_shape=[jax.ShapeDtypeStruct((nb, s, U_COLS), F32),
                   jax.ShapeDtypeStruct((nb, s, G_COLS), BF16),
                   jax.ShapeDtypeStruct((nb, s, D_ATT), BF16),
                   fm_shape, fm_shape, fm_shape],
        compiler_params=_cparams(("arbitrary", "arbitrary")),
        name="inproj",
    )(x, scale, shift, g.reshape(1, D_MODEL), w_row, w_t, cos_r, sin_r, cos_t, sin_t)


def _outproj_kernel(x_ref, as_ref, att_ref, gate_ref, gm_ref, w_ref, fg_ref, o_ref, *, final):
    g = gate_ref[0].astype(F32)
    a_s = as_ref[0].astype(F32)
    mixed = jnp.concatenate([a_s[:, :D_LRU] * g[:, :D_LRU],
                             att_ref[0].astype(F32) * g[:, D_LRU:D_LRU + D_ATT],
                             a_s[:, D_LRU:] * g[:, D_LRU + D_ATT:]], axis=1).astype(BF16)
    y = jnp.dot(mixed, w_ref[...], preferred_element_type=F32)
    xn = x_ref[0] + gm_ref[0] * y
    if final:
        xn = xn * lax.rsqrt(jnp.mean(xn * xn, axis=-1, keepdims=True) + EPS) * fg_ref[...]
    o_ref[0] = xn


def _outproj(x, as_out, att, gates, gate_mod, w_out, final_g, tm, final):
    nb, s, _ = x.shape
    rm = gate_mod.shape[1]
    mod_block = (1, tm, D_MODEL) if rm == s else (1, 1, D_MODEL)
    mod_map = (lambda b, i: (b, i, 0)) if rm == s else (lambda b, i: (b, 0, 0))
    row = lambda n: pl.BlockSpec((1, tm, n), lambda b, i: (b, i, 0))
    return pl.pallas_call(
        functools.partial(_outproj_kernel, final=final),
        grid=(nb, s // tm),
        in_specs=[row(D_MODEL), row(U_COLS), row(D_ATT), row(G_COLS),
                  pl.BlockSpec(mod_block, mod_map),
                  pl.BlockSpec((D_MIX, D_MODEL), lambda b, i: (0, 0)),
                  pl.BlockSpec((1, D_MODEL), lambda b, i: (0, 0))],
        out_specs=row(D_MODEL),
        out_shape=jax.ShapeDtypeStruct((nb, s, D_MODEL), F32),
        compiler_params=_cparams(("arbitrary", "arbitrary")),
        name="outproj",
    )(x, as_out, att, gates, gate_mod, w_out, final_g.reshape(1, D_MODEL))


def _lru_gates(xc, wg_ref, bg_ref, c8_ref):
    gates = jnp.dot(xc.astype(BF16), wg_ref[...], preferred_element_type=F32) + bg_ref[...]
    r = jax.nn.sigmoid(gates[:, :D_LRU])
    i = jax.nn.sigmoid(gates[:, D_LRU:])
    log_a = r * c8_ref[...]
    a = jnp.exp(log_a)
    mult = jnp.sqrt(1.0 - jnp.exp(2.0 * log_a))
    return a, mult * i * xc


def _s5_out(x_cat_bf16, sx, cbd_ref, d_ref, wglu_ref, bglu_ref):
    y = jnp.dot(x_cat_bf16, cbd_ref[...], preferred_element_type=F32) + d_ref[...] * sx
    z = jax.nn.gelu(y)
    return z * jax.nn.sigmoid(jnp.dot(z.astype(BF16), wglu_ref[...], preferred_element_type=F32)
                              + bglu_ref[...])


def _rec_prompt_kernel(u_ref, cw_ref, cb_ref, wg_ref, bg_ref, c8_ref, lbr_ref, lbi_ref,
                       bbd_ref, cbd_ref, d_ref, wglu_ref, bglu_ref,
                       o_ref, hl_ref, cbuf_ref, sre_ref, sim_ref,
                       ext, a_s, b_s, h_s, bu_s, x_s, h_st, x_st, *, nb, tc, pitch):
    c = pl.program_id(0)
    nl = D_LRU // LANES
    ns = S5_P // LANES

    @pl.when(c == 0)
    def _():
        ext[:, 0:8, :] = jnp.zeros((nb, 8, D_LRU), F32)
        h_st[...] = jnp.zeros(h_st.shape, F32)
        x_st[...] = jnp.zeros(x_st.shape, F32)

    for b in range(nb):
        r0 = b * pitch
        ext[b, 8:8 + tc, :] = u_ref[b, :, 0:D_LRU]
        xc = cb_ref[...]
        for j in range(CONV_W):
            xc = xc + ext[b, 8 - (CONV_W - 1) + j:8 - (CONV_W - 1) + j + tc, :] * cw_ref[j:j + 1, :]
        ext[b, 8 - (CONV_W - 1):8, :] = ext[b, 8 + tc - (CONV_W - 1):8 + tc, :]
        a, bb = _lru_gates(xc, wg_ref, bg_ref, c8_ref)
        for j in range(nl):
            a_s[j, r0:r0 + tc, :] = a[:, j * LANES:(j + 1) * LANES]
            b_s[j, r0:r0 + tc, :] = bb[:, j * LANES:(j + 1) * LANES]
        sxb = u_ref[b, :, D_LRU:U_COLS].astype(BF16)
        for j in range(ns):
            bu = jnp.dot(sxb, bbd_ref[:, 2 * j * LANES:(2 * j + 2) * LANES], preferred_element_type=F32)
            bu_s[j, r0:r0 + tc, :] = bu[:, :LANES]
            bu_s[ns + j, r0:r0 + tc, :] = bu[:, LANES:]

    def step(t, carry):
        h, xr, xi = carry
        rows = pl.ds(t, nb, stride=pitch)
        h_new, xr_new, xi_new = [], [], []
        for j in range(nl):
            hj = a_s[j, rows, :] * h[j] + b_s[j, rows, :]
            h_s[j, rows, :] = hj
            h_new.append(hj)
        for j in range(ns):
            lr = lbr_ref[j]
            li = lbi_ref[j]
            nr = lr * xr[j] - li * xi[j] + bu_s[j, rows, :]
            ni = lr * xi[j] + li * xr[j] + bu_s[ns + j, rows, :]
            x_s[j, rows, :] = nr
            x_s[ns + j, rows, :] = ni
            xr_new.append(nr)
            xi_new.append(ni)
        return tuple(h_new), tuple(xr_new), tuple(xi_new)

    init = (tuple(h_st[j] for j in range(nl)),
            tuple(x_st[j] for j in range(ns)),
            tuple(x_st[ns + j] for j in range(ns)))
    h, xr, xi = lax.fori_loop(0, tc, step, init)
    for j in range(nl):
        h_st[j] = h[j]
        hl_ref[:, j * LANES:(j + 1) * LANES] = h[j]
    for j in range(ns):
        x_st[j] = xr[j]
        x_st[ns + j] = xi[j]
        sre_ref[:, j * LANES:(j + 1) * LANES] = xr[j]
        sim_ref[:, j * LANES:(j + 1) * LANES] = xi[j]

    for b in range(nb):
        r0 = b * pitch
        for j in range(nl):
            o_ref[b, :, j * LANES:(j + 1) * LANES] = h_s[j, r0:r0 + tc, :].astype(o_ref.dtype)
        x_cat = jnp.concatenate([x_s[j, r0:r0 + tc, :].astype(BF16) for j in range(2 * ns)], axis=1)
        o_ref[b, :, D_LRU:U_COLS] = _s5_out(x_cat, u_ref[b, :, D_LRU:U_COLS], cbd_ref, d_ref,
                                            wglu_ref, bglu_ref).astype(o_ref.dtype)
        cbuf_ref[b] = ext[b, 8 - (CONV_W - 1):8, :]


def _rec_prompt(u, pr, tc):
    nb, s, _ = u.shape
    pitch = tc + 8
    nl = D_LRU // LANES
    ns = S5_P // LANES
    full = lambda a: pl.BlockSpec(a.shape, lambda c, _n=a.ndim: (0,) * _n)
    params = [pr["cw"], pr["cb"], pr["wg"], pr["bg"], pr["c8"], pr["lbr"], pr["lbi"],
              pr["bbd"], pr["cbd"], pr["d"], pr["wglu"], pr["bglu"]]
    st = lambda *shape: pl.BlockSpec(shape, lambda c, _n=len(shape): (0,) * _n)
    return pl.pallas_call(
        functools.partial(_rec_prompt_kernel, nb=nb, tc=tc, pitch=pitch),
        grid=(s // tc,),
        in_specs=[pl.BlockSpec((nb, tc, U_COLS), lambda c: (0, c, 0))] + [full(a) for a in params],
        out_specs=[pl.BlockSpec((nb, tc, U_COLS), lambda c: (0, c, 0)),
                   st(nb, D_LRU), st(nb, CONV_W - 1, D_LRU), st(nb, S5_P), st(nb, S5_P)],
        out_shape=[jax.ShapeDtypeStruct((nb, s, U_COLS), BF16),
                   jax.ShapeDtypeStruct((nb, D_LRU), F32),
                   jax.ShapeDtypeStruct((nb, CONV_W - 1, D_LRU), F32),
                   jax.ShapeDtypeStruct((nb, S5_P), F32),
                   jax.ShapeDtypeStruct((nb, S5_P), F32)],
        scratch_shapes=[pltpu.VMEM((nb, tc + 8, D_LRU), F32),
                        pltpu.VMEM((nl, nb * pitch, LANES), F32),
                        pltpu.VMEM((nl, nb * pitch, LANES), F32),
                        pltpu.VMEM((nl, nb * pitch, LANES), F32),
                        pltpu.VMEM((2 * ns, nb * pitch, LANES), F32),
                        pltpu.VMEM((2 * ns, nb * pitch, LANES), F32),
                        pltpu.VMEM((nl, nb, LANES), F32),
                        pltpu.VMEM((2 * ns, nb, LANES), F32)],
        compiler_params=_cparams(("arbitrary",)),
        name="rec_prompt",
    )(u, *params)


def _rec_sample_kernel(u_ref, h0_ref, buf0_ref, s0r_ref, s0i_ref,
                       cw_ref, cb_ref, wg_ref, bg_ref, c8_ref, lbr_ref, lbi_ref,
                       bbd_ref, cbd_ref, d_ref, wglu_ref, bglu_ref,
                       o_ref, hl_ref, cbuf_ref, sre_ref, sim_ref, *, nt):
    ns = S5_P // LANES
    ext = [buf0_ref[j] for j in range(CONV_W - 1)] + [u_ref[t, :, 0:D_LRU] for t in range(nt)]
    xcs = []
    for t in range(nt):
        xc = cb_ref[...]
        for j in range(CONV_W):
            xc = xc + ext[t + j] * cw_ref[j:j + 1, :]
        xcs.append(xc)
    a, bb = _lru_gates(jnp.concatenate(xcs, axis=0), wg_ref, bg_ref, c8_ref)
    nbat = h0_ref.shape[0]
    h = h0_ref[...]
    for t in range(nt):
        h = a[t * nbat:(t + 1) * nbat] * h + bb[t * nbat:(t + 1) * nbat]
        o_ref[t, :, 0:D_LRU] = h.astype(o_ref.dtype)
    hl_ref[...] = h
    for j in range(CONV_W - 1):
        cbuf_ref[j] = ext[nt + j]

    sx = jnp.concatenate([u_ref[t, :, D_LRU:U_COLS] for t in range(nt)], axis=0)
    bu = jnp.dot(sx.astype(BF16), bbd_ref[...], preferred_element_type=F32)
    lr = jnp.concatenate([lbr_ref[j] for j in range(ns)], axis=1)
    li = jnp.concatenate([lbi_ref[j] for j in range(ns)], axis=1)
    xr = s0r_ref[...]
    xi = s0i_ref[...]
    xs = []
    for t in range(nt):
        but = bu[t * nbat:(t + 1) * nbat]
        bur = jnp.concatenate([but[:, 2 * j * LANES:(2 * j + 1) * LANES] for j in range(ns)], axis=1)
        bui = jnp.concatenate([but[:, (2 * j + 1) * LANES:(2 * j + 2) * LANES] for j in range(ns)], axis=1)
        xr, xi = lr * xr - li * xi + bur, lr * xi + li * xr + bui
        xs.append(jnp.concatenate([xr, xi], axis=1).astype(BF16))
    sre_ref[...] = xr
    sim_ref[...] = xi
    s_out = _s5_out(jnp.concatenate(xs, axis=0), sx, cbd_ref, d_ref, wglu_ref, bglu_ref)
    for t in range(nt):
        o_ref[t, :, D_LRU:U_COLS] = s_out[t * nbat:(t + 1) * nbat].astype(o_ref.dtype)


def _rec_sample(u_t, h0, buf0_t, s0r, s0i, pr):
    nt, nbat, _ = u_t.shape
    params = [pr["cw"], pr["cb"], pr["wg"], pr["bg"], pr["c8"], pr["lbr"], pr["lbi"],
              pr["bbd"], pr["cbd"], pr["d"], pr["wglu"], pr["bglu"]]
    return pl.pallas_call(
        functools.partial(_rec_sample_kernel, nt=nt),
        out_shape=[jax.ShapeDtypeStruct((nt, nbat, U_COLS), BF16),
                   jax.ShapeDtypeStruct((nbat, D_LRU), F32),
                   jax.ShapeDtypeStruct((CONV_W - 1, nbat, D_LRU), F32),
                   jax.ShapeDtypeStruct((nbat, S5_P), F32),
                   jax.ShapeDtypeStruct((nbat, S5_P), F32)],
        compiler_params=pltpu.CompilerParams(vmem_limit_bytes=VMEM_LIMIT),
        name="rec_sample",
    )(u_t, h0, buf0_t, s0r, s0i, *params)


def _rec_params(lru_conv_w, lru_conv_b, lru_w_r, lru_b_r, lru_w_i, lru_b_i, lru_lam,
                s5_a_re, s5_a_im, s5_log_dt, s5_b_re, s5_b_im, s5_c_re, s5_c_im, s5_d, s5_w_glu, s5_b_glu):
    ns = S5_P // LANES
    bd = lambda w: jax.scipy.linalg.block_diag(*[w[n] for n in range(w.shape[0])])
    wg = jnp.concatenate([bd(lru_w_r), bd(lru_w_i)], axis=1).astype(BF16)
    bg = jnp.concatenate([lru_b_r, lru_b_i]).reshape(1, 2 * D_LRU)
    c8 = (LRU_C * jax.nn.log_sigmoid(lru_lam.astype(F32))).reshape(1, D_LRU)
    dt = jnp.exp(s5_log_dt.astype(F32))[:, None]
    ar = s5_a_re.astype(F32)
    ai = s5_a_im.astype(F32)
    mag = jnp.exp(ar * dt)
    lb_re = mag * jnp.cos(ai * dt)
    lb_im = mag * jnp.sin(ai * dt)
    den = ar * ar + ai * ai
    n_re = lb_re - 1.0
    co_re = (n_re * ar + lb_im * ai) / den
    co_im = (lb_im * ar - n_re * ai) / den
    bb_re = co_re[..., None] * s5_b_re - co_im[..., None] * s5_b_im
    bb_im = co_re[..., None] * s5_b_im + co_im[..., None] * s5_b_re
    bre = bd(jnp.swapaxes(bb_re, 1, 2))
    bim = bd(jnp.swapaxes(bb_im, 1, 2))
    bbd = jnp.stack([bre.reshape(D_S5, ns, LANES), bim.reshape(D_S5, ns, LANES)], axis=2)
    bbd = bbd.reshape(D_S5, 2 * S5_P).astype(BF16)
    cre = bd(jnp.swapaxes(s5_c_re, 1, 2))
    cim = bd(jnp.swapaxes(s5_c_im, 1, 2))
    cbd = jnp.concatenate([cre, -cim], axis=0).astype(BF16)
    return dict(cw=lru_conv_w, cb=lru_conv_b.reshape(1, D_LRU), wg=wg, bg=bg, c8=c8,
                lbr=lb_re.reshape(ns, 1, LANES), lbi=lb_im.reshape(ns, 1, LANES),
                bbd=bbd, cbd=cbd, d=s5_d.reshape(1, D_S5),
                wglu=s5_w_glu.astype(BF16), bglu=s5_b_glu.reshape(1, D_S5))


def _moba_prompt_kernel(pt_ref, qt_ref, krm_ref, kt_ref, vt_ref, cache_ref, o_ref, ks_ref,
                        km_s, vb_s, bias_s, sc_s, pbuf, psem, *, s, nblk, grp, nh, layer, pg, nsteps):
    qi = pl.program_id(2)
    step = (pl.program_id(0) * (N_HEADS // nh) + pl.program_id(1)) * nblk + qi
    tq = MOBA_BLOCK
    hd = HEAD_DIM
    gk = grp * tq

    @pl.when(qi == 0)
    def _():
        r = lax.broadcasted_iota(jnp.int32, (nblk, s), 1) // MOBA_BLOCK
        n = lax.broadcasted_iota(jnp.int32, (nblk, s), 0)
        pm = jnp.where(r == n, 1.0 / MOBA_BLOCK, 0.0).astype(F32)
        for hh in range(nh):
            km_s[hh] = lax.dot_general(pm, kt_ref[0, hh * hd:(hh + 1) * hd, :], NT_DIMS,
                                       precision=HIGHEST, preferred_element_type=F32)
        ones_row = (lax.broadcasted_iota(jnp.int32, (VB_ROWS - hd, gk), 0) == 0).astype(BF16)
        for g in range(nblk // grp):
            for hh in range(nh):
                vb_s[g, hh, 0:hd, :] = vt_ref[0, hh * hd:(hh + 1) * hd, g * gk:(g + 1) * gk].astype(BF16)
                vb_s[g, hh, hd:VB_ROWS, :] = ones_row

    qt = qt_ref[0]
    qs = (qt * (hd ** -0.5 * LOG2E)).astype(BF16)
    feat = lax.broadcasted_iota(jnp.int32, (nh * hd, tq), 0)
    blk_iota = lax.broadcasted_iota(jnp.int32, (nblk, tq), 0)
    past = blk_iota < qi
    ws = []
    for hh in range(nh):
        ws.append(jnp.where((feat >= hh * hd) & (feat < (hh + 1) * hd), qs, jnp.zeros_like(qs)))
        sb = jnp.dot(km_s[hh], qt[hh * hd:(hh + 1) * hd, :], precision=HIGHEST,
                     preferred_element_type=F32)
        sb = jnp.where(past, sb, -jnp.inf)
        cnt = jnp.zeros((nblk, tq), jnp.int32)
        for i in range(nblk):
            ri = sb[i:i + 1, :]
            beats = (ri > sb) | ((ri == sb) & (i < blk_iota))
            cnt = cnt + beats.astype(jnp.int32)
        sel = (cnt < MOBA_TOPK) & past
        bias_s[hh] = jnp.where(sel | (blk_iota == qi), 0.0, NEG).astype(F32)

    def scores(g, hh):
        st = pl.multiple_of(g * gk, gk)
        sc = jnp.dot(krm_ref[0, pl.ds(st, gk), :], ws[hh], preferred_element_type=F32)
        return jnp.concatenate(
            [sc[i * tq:(i + 1) * tq] + bias_s[hh, pl.ds(g * grp + i, 1), :] for i in range(grp)], axis=0)

    def pass1(g, ms):
        st = pl.multiple_of(g * gk, gk)
        out = []
        for hh in range(nh):
            sc = scores(g, hh)
            sc_s[hh, pl.ds(st, gk), :] = sc
            out.append(jnp.maximum(ms[hh], jnp.max(sc, axis=0, keepdims=True)))
        return tuple(out)

    g_own = qi // grp
    ms = lax.fori_loop(0, g_own, pass1, tuple(jnp.full((1, tq), NEG, F32) for _ in range(nh)))

    _ksum_fetch(pt_ref, cache_ref, ks_ref, pbuf, psem, step, layer=layer, nsteps=nsteps, pg=pg)
    half = N_HEADS // 2

    r_own = qi % grp
    st_own = pl.multiple_of(g_own * gk, gk)
    causal = (lax.broadcasted_iota(jnp.int32, (tq, tq), 0) <= lax.broadcasted_iota(jnp.int32, (tq, tq), 1))

    def own_pass1(nb_own):
        rows = nb_own * tq

        def run(ms):
            _ksum_merge(ks_ref, pbuf, step, range(0, half), pg=pg)
            out = []
            for hh in range(nh):
                sc = jnp.dot(krm_ref[0, pl.ds(st_own, rows), :], ws[hh], preferred_element_type=F32)
                parts = [sc[i * tq:(i + 1) * tq] + bias_s[hh, pl.ds(g_own * grp + i, 1), :]
                         for i in range(nb_own - 1)]
                parts.append(jnp.where(causal, sc[rows - tq:rows], NEG))
                sc = jnp.concatenate(parts, axis=0)
                sc_s[hh, pl.ds(st_own, rows), :] = sc
                out.append(jnp.maximum(ms[hh], jnp.max(sc, axis=0, keepdims=True)))
            return tuple(out)
        return run

    ms = lax.switch(r_own, [own_pass1(i + 1) for i in range(grp)], ms)

    def pass2_group(g, rows, accs):
        st = pl.multiple_of(g * gk, gk)
        out = []
        for hh in range(nh):
            p = jnp.exp2(sc_s[hh, pl.ds(st, rows), :] - ms[hh])
            out.append(accs[hh] + jnp.dot(vb_s[g, hh, :, 0:rows], p.astype(BF16), preferred_element_type=F32))
        return tuple(out)

    accs = lax.fori_loop(0, g_own, lambda g, a: pass2_group(g, gk, a),
                         tuple(jnp.zeros((VB_ROWS, tq), F32) for _ in range(nh)))
    def own_pass2(nb_own):
        def run(accs):
            _ksum_merge(ks_ref, pbuf, step, range(half, N_HEADS), pg=pg)
            return pass2_group(g_own, nb_own * tq, accs)
        return run

    accs = lax.switch(r_own, [own_pass2(i + 1) for i in range(grp)], accs)
    out_t = jnp.concatenate([acc[0:hd] / acc[hd:hd + 1] for acc in accs], axis=0)
    o_ref[0] = jnp.transpose(out_t).astype(o_ref.dtype)


def _moba_prompt(qt, krm, kt, vt, pt_flat, cache_kt, layer):
    nb, _, s = qt.shape
    nblk = s // MOBA_BLOCK
    tq = MOBA_BLOCK
    nh = 4
    grp = 4
    hp = nh * HEAD_DIM
    nhp = N_HEADS // nh
    nsteps = nb * nhp * nblk
    pg = pt_flat.shape[0] // nsteps
    nbat = pt_flat.shape[0] // N_PAGES
    steps_per_b = N_PAGES // pg
    once = dict(pipeline_mode=pl.Buffered(1))
    return pl.pallas_call(
        functools.partial(_moba_prompt_kernel, s=s, nblk=nblk, grp=grp, nh=nh, layer=layer, pg=pg,
                          nsteps=nsteps),
        grid_spec=pltpu.PrefetchScalarGridSpec(
            num_scalar_prefetch=1,
            grid=(nb, nhp, nblk),
            in_specs=[pl.BlockSpec((1, hp, tq), lambda b, h, i, pt: (b, h, i)),
                      pl.BlockSpec((1, s, hp), lambda b, h, i, pt: (b, 0, h), **once),
                      pl.BlockSpec((1, hp, s), lambda b, h, i, pt: (b, h, 0), **once),
                      pl.BlockSpec((1, hp, s), lambda b, h, i, pt: (b, h, 0), **once),
                      pl.BlockSpec(memory_space=pl.ANY)],
            out_specs=[pl.BlockSpec((1, tq, hp), lambda b, h, i, pt: (b, i, h)),
                       pl.BlockSpec((1, N_HEADS, HEAD_DIM, LANES),
                                    lambda b, h, i, pt: (((b * nhp + h) * nblk + i) // steps_per_b, 0, 0, 0))],
            scratch_shapes=[pltpu.VMEM((nh, nblk, HEAD_DIM), F32),
                            pltpu.VMEM((nblk // grp, nh, VB_ROWS, grp * tq), BF16),
                            pltpu.VMEM((nh, nblk, tq), F32),
                            pltpu.VMEM((nh, s, tq), F32),
                            pltpu.VMEM((2, pg, N_HEADS, HEAD_DIM, PAGE_SIZE), F32),
                            pltpu.SemaphoreType.DMA((2, pg))]),
        out_shape=[jax.ShapeDtypeStruct((nb, s, D_ATT), BF16),
                   jax.ShapeDtypeStruct((nbat, N_HEADS, HEAD_DIM, LANES), F32)],
        compiler_params=_cparams(("arbitrary", "arbitrary", "arbitrary")),
        name="moba_prompt",
    )(pt_flat, qt, krm, kt, vt, cache_kt)


PAGES_PER_BLOCK = MOBA_BLOCK // PAGE_SIZE
N_PAGES = PAST_LEN // PAGE_SIZE
N_FULL = PAST_LEN // MOBA_BLOCK
SAMPLE_SLOTS = 3


def _ksum_fetch(pt_ref, cache_ref, o_ref, buf, sem, n, *, layer, nsteps, pg):
    slot = n % 2
    g = n % (N_PAGES // pg)

    def start(step, sl):
        for j in range(pg):
            pid = pt_ref[step * pg + j]
            pltpu.make_async_copy(cache_ref.at[layer, pid], buf.at[sl, j], sem.at[sl, j]).start()

    @pl.when(n == 0)
    def _():
        start(0, 0)

    @pl.when(n + 1 < nsteps)
    def _():
        start(n + 1, 1 - slot)

    @pl.when(g == 0)
    def _():
        o_ref[...] = jnp.zeros(o_ref.shape, F32)

    for j in range(pg):
        pltpu.make_async_copy(buf.at[slot, j], buf.at[slot, j], sem.at[slot, j]).wait()


def _ksum_merge(o_ref, buf, n, heads, *, pg):
    slot = n % 2
    g = n % (N_PAGES // pg)
    bpg = pg // PAGES_PER_BLOCK
    lane = lax.broadcasted_iota(jnp.int32, (HEAD_DIM, LANES), 1)
    for hh in heads:
        acc = o_ref[0, hh]
        for blk in range(bpg):
            tile = buf[slot, PAGES_PER_BLOCK * blk, hh]
            for pp in range(1, PAGES_PER_BLOCK):
                tile = tile + buf[slot, PAGES_PER_BLOCK * blk + pp, hh]
            acc = jnp.where(lane == g * bpg + blk, jnp.sum(tile, axis=1, keepdims=True), acc)
        o_ref[0, hh] = acc


def _topk_kernel(q_ref, ks_ref, o_ref):
    nq = q_ref.shape[2]
    lane = lax.broadcasted_iota(jnp.int32, (nq, LANES), 1)
    for hh in range(N_HEADS):
        km = ks_ref[0, hh] * (1.0 / MOBA_BLOCK)
        sb = jnp.dot(q_ref[0, hh], km, precision=HIGHEST, preferred_element_type=F32)
        sb = jnp.where(lane < N_FULL, sb, -jnp.inf)
        out = jnp.zeros((nq, LANES), jnp.int32)
        for r in range(MOBA_TOPK):
            mx = jnp.max(sb, axis=1, keepdims=True)
            idx = jnp.min(jnp.where(sb == mx, lane, LANES), axis=1, keepdims=True)
            out = jnp.where(lane == r, idx, out)
            sb = jnp.where(lane == idx, -jnp.inf, sb)
        o_ref[0, hh] = out


def _sample_topk(q_pad, ksum_t):
    nbat, nh, nq, hd = q_pad.shape
    return pl.pallas_call(
        _topk_kernel,
        grid=(nbat,),
        in_specs=[pl.BlockSpec((1, nh, nq, hd), lambda b: (b, 0, 0, 0)),
                  pl.BlockSpec((1, nh, hd, LANES), lambda b: (b, 0, 0, 0))],
        out_specs=pl.BlockSpec((1, nh, nq, LANES), lambda b: (b, 0, 0, 0)),
        out_shape=jax.ShapeDtypeStruct((nbat, nh, nq, LANES), jnp.int32),
        compiler_params=_cparams(("arbitrary",)),
        name="sample_topk",
    )(q_pad, ksum_t)


def _slab_copy(src, buf, sem, slot, j):
    return pltpu.make_async_copy(src, buf.at[slot, j], sem.at[slot, j])


def _moba_sample_kernel(idx_ref, pt_ref, q_ref, kn_ref, vn_ref, ck_ref, cv_ref, o_ref,
                        kbuf, vbuf, ksem, vsem, *, layer, nsteps, nt, hps):
    n = pl.program_id(0)
    slot = n % SAMPLE_SLOTS
    b = n // (N_HEADS // hps)
    nsel = MOBA_TOPK * PAGES_PER_BLOCK
    nslab = nt * nsel
    ahead = SAMPLE_SLOTS - 1

    def start(step, sl):
        sb = step // (N_HEADS // hps)
        for e in range(hps):
            bh = step * hps + e
            sh = bh % N_HEADS
            for t in range(nt):
                for r in range(MOBA_TOPK):
                    blk = idx_ref[(bh * nt + t) * MOBA_TOPK + r]
                    for pp in range(PAGES_PER_BLOCK):
                        pid = pt_ref[sb * N_PAGES + blk * PAGES_PER_BLOCK + pp]
                        j = e * nslab + (t * MOBA_TOPK + r) * PAGES_PER_BLOCK + pp
                        _slab_copy(ck_ref.at[layer, pid, sh], kbuf, ksem, sl, j).start()
                        _slab_copy(cv_ref.at[layer, pid, sh], vbuf, vsem, sl, j).start()

    def wait(sl):
        for j in range(hps * nslab):
            _slab_copy(kbuf.at[sl, j], kbuf, ksem, sl, j).wait()
            _slab_copy(vbuf.at[sl, j], vbuf, vsem, sl, j).wait()

    @pl.when(n == 0)
    def _():
        for s0 in range(ahead):
            start(s0, s0)

    @pl.when(n + ahead < nsteps)
    def _():
        start(n + ahead, (n + ahead) % SAMPLE_SLOTS)

    wait(slot)

    nq = q_ref.shape[2]
    ntok = kn_ref.shape[2]
    nkeys = nsel * PAGE_SIZE
    qrow = lax.broadcasted_iota(jnp.int32, (nq, nkeys), 0)
    orow = lax.broadcasted_iota(jnp.int32, (nq, ntok), 0)
    opos = lax.broadcasted_iota(jnp.int32, (nq, ntok), 1) - b * nt
    own_ok = (opos >= 0) & (opos <= orow)
    for e in range(hps):
        qs = (q_ref[0, e] * (HEAD_DIM ** -0.5)).astype(BF16)
        base = e * nslab
        scores = []
        for t in range(nt):
            kt = jnp.concatenate([kbuf[slot, base + t * nsel + i] for i in range(nsel)], axis=1).astype(BF16)
            st = jnp.dot(qs, kt, preferred_element_type=F32)
            scores.append(jnp.where(qrow == t, st, -jnp.inf))
        so = jnp.dot(qs, kn_ref[e].astype(BF16), preferred_element_type=F32)
        so = jnp.where(own_ok, so, -jnp.inf)
        m = jnp.max(so, axis=1, keepdims=True)
        for st in scores:
            m = jnp.maximum(m, jnp.max(st, axis=1, keepdims=True))
        po = jnp.exp(so - m)
        den = jnp.sum(po, axis=1, keepdims=True)
        acc = lax.dot_general(po.astype(BF16), vn_ref[e].astype(BF16), NT_DIMS, preferred_element_type=F32)
        for t in range(nt):
            pt_ = jnp.exp(scores[t] - m)
            den = den + jnp.sum(pt_, axis=1, keepdims=True)
            vt = jnp.concatenate([vbuf[slot, base + t * nsel + i] for i in range(nsel)], axis=1).astype(BF16)
            acc = acc + lax.dot_general(pt_.astype(BF16), vt, NT_DIMS, preferred_element_type=F32)
        o_ref[0, e] = acc / den


def _moba_sample(q_pad, kt_new, vt_new, cache_kt, cache_vt, idx_flat, pt_flat, layer, nt):
    nbat, nh, nq, hd = q_pad.shape
    ntok = kt_new.shape[1]
    hps = 2
    npair = nh // hps
    nsteps = nbat * npair
    nslab = hps * nt * MOBA_TOPK * PAGES_PER_BLOCK
    qtok = pl.BlockSpec((1, hps, nq, hd), lambda n, idx, pt: (n // npair, n % npair, 0, 0))
    new = pl.BlockSpec((hps, hd, ntok), lambda n, idx, pt: (n % npair, 0, 0))
    return pl.pallas_call(
        functools.partial(_moba_sample_kernel, layer=layer, nsteps=nsteps, nt=nt, hps=hps),
        grid_spec=pltpu.PrefetchScalarGridSpec(
            num_scalar_prefetch=2,
            grid=(nsteps,),
            in_specs=[qtok, new, new, pl.BlockSpec(memory_space=pl.ANY), pl.BlockSpec(memory_space=pl.ANY)],
            out_specs=qtok,
            scratch_shapes=[pltpu.VMEM((SAMPLE_SLOTS, nslab, hd, PAGE_SIZE), F32),
                            pltpu.VMEM((SAMPLE_SLOTS, nslab, hd, PAGE_SIZE), F32),
                            pltpu.SemaphoreType.DMA((SAMPLE_SLOTS, nslab)),
                            pltpu.SemaphoreType.DMA((SAMPLE_SLOTS, nslab))]),
        out_shape=jax.ShapeDtypeStruct((nbat, nh, nq, hd), F32),
        compiler_params=_cparams(("arbitrary",)),
        name="moba_sample",
    )(idx_flat, pt_flat, q_pad, kt_new.reshape(nh, hd, ntok), vt_new.reshape(nh, hd, ntok),
      cache_kt, cache_vt)


def _rope_tables(pos):
    inv = jnp.power(ROPE_THETA, -jnp.arange(HALF, dtype=F32) / HALF)
    ang = pos.astype(F32)[:, None] * inv[None, :]
    cos, sin = jnp.cos(ang), jnp.sin(ang)
    reps = LANES // HEAD_DIM
    return (jnp.tile(jnp.concatenate([cos, cos], axis=1), (1, reps)),
            jnp.tile(jnp.concatenate([-sin, sin], axis=1), (1, reps)),
            cos.T, sin.T)


def _split_w_in(w):
    a_x, a_g, q, k, v, b_g, s_x, s_g = jnp.split(
        w, (D_LRU, 2 * D_LRU, 2 * D_LRU + D_ATT, 2 * D_LRU + 2 * D_ATT, 2 * D_LRU + 3 * D_ATT,
            2 * D_LRU + 4 * D_ATT, 2 * D_LRU + 4 * D_ATT + D_S5), axis=1)
    w_row = jnp.concatenate([a_x, s_x, a_g, b_g, s_g, k], axis=1).astype(BF16)
    w_t = jnp.concatenate([q, k, v], axis=1).T.astype(BF16)
    return w_row, w_t


def kernel(x_prompt, x_sample, cache_k, cache_v, state_lru_h, state_lru_conv, state_s5_re, state_s5_im,
           page_table, c_prompt, c_sample, norm_g, w_ada, b_ada, w_in, w_out, lru_conv_w, lru_conv_b,
           lru_w_r, lru_b_r, lru_w_i, lru_b_i, lru_lam, s5_a_re, s5_a_im, s5_log_dt, s5_b_re, s5_b_im,
           s5_c_re, s5_c_im, s5_d, s5_w_glu, s5_b_glu, final_g):
    nbp, seq, _ = x_prompt.shape
    nbs, nts, _ = x_sample.shape
    ntok_s = nbs * nts

    rope_p = _rope_tables(jnp.arange(seq, dtype=jnp.int32))
    rope_s = _rope_tables(PAST_LEN + (jnp.arange(ntok_s, dtype=jnp.int32) % nts))

    n_c = nbp + nbs
    c_all = jnp.concatenate([c_prompt, c_sample, jnp.zeros((-n_c % 8, D_MODEL), F32)], axis=0)
    mod = _modulation(c_all, w_ada, b_ada)

    cache_kt = jnp.swapaxes(cache_k, 3, 4)
    cache_vt = jnp.swapaxes(cache_v, 3, 4)
    pt_flat = page_table.reshape(-1)
    xp = x_prompt
    xs = x_sample.reshape(1, ntok_s, D_MODEL)
    outs_p = [[] for _ in range(6)]
    outs_s = [[] for _ in range(6)]
    for l in range(DEPTH):
        last = l == DEPTH - 1
        w_row, w_t = _split_w_in(w_in[l])
        w_o = w_out[l].astype(BF16)
        pr = _rec_params(lru_conv_w[l], lru_conv_b[l], lru_w_r[l], lru_b_r[l], lru_w_i[l], lru_b_i[l],
                         lru_lam[l], s5_a_re[l], s5_a_im[l], s5_log_dt[l], s5_b_re[l], s5_b_im[l],
                         s5_c_re[l], s5_c_im[l], s5_d[l], s5_w_glu[l], s5_b_glu[l])
        shift, scale, gate = jnp.split(mod[l], 3, axis=-1)

        mp = lambda a: a[:nbp].reshape(nbp, 1, D_MODEL)
        u, gates, krm, qt, kt, vt = _inproj(xp, mp(scale), mp(shift), norm_g[l], w_row, w_t, rope_p, tm=512)
        as_out, h_last, cbuf, s_re, s_im = _rec_prompt(u, pr, tc=256)
        att, ksum_t = _moba_prompt(qt, krm, kt, vt, pt_flat, cache_kt, l)
        xp = _outproj(xp, as_out, att, gates, mp(gate), w_o, final_g, tm=512, final=last)
        for lst, val in zip(outs_p, (kt.reshape(nbp, N_HEADS, HEAD_DIM, seq), vt.reshape(nbp, N_HEADS, HEAD_DIM, seq),
                                     h_last, cbuf, s_re.reshape(nbp, S5_GROUPS, S5_STATE),
                                     s_im.reshape(nbp, S5_GROUPS, S5_STATE))):
            lst.append(val)

        ms = lambda a: jnp.repeat(a[nbp:n_c], nts, axis=0).reshape(1, ntok_s, D_MODEL)
        u, gates, _, qt, kt, vt = _inproj(xs, ms(scale), ms(shift), norm_g[l], w_row, w_t, rope_s, tm=ntok_s)
        to_heads = lambda a: a.reshape(N_HEADS, HEAD_DIM, nbs, nts).transpose(2, 0, 3, 1)
        q, k, v = to_heads(qt), to_heads(kt), to_heads(vt)
        u_t = u.reshape(nbs, nts, U_COLS).transpose(1, 0, 2)
        as_t, h_last, cbuf_t, s_re, s_im = _rec_sample(
            u_t, state_lru_h[l], state_lru_conv[l].transpose(1, 0, 2),
            state_s5_re[l].reshape(nbs, S5_P), state_s5_im[l].reshape(nbs, S5_P), pr)
        as_out = as_t.transpose(1, 0, 2).reshape(1, ntok_s, U_COLS)
        q_pad = jnp.pad(q, ((0, 0), (0, 0), (0, -nts % 8), (0, 0)))
        idx = _sample_topk(q_pad, ksum_t)[:, :, :nts, :MOBA_TOPK].reshape(-1)
        att = _moba_sample(q_pad, kt[0], vt[0], cache_kt, cache_vt, idx, pt_flat, l, nts)[:, :, :nts]
        att = att.transpose(0, 2, 1, 3).reshape(1, ntok_s, D_ATT).astype(BF16)
        xs = _outproj(xs, as_out, att, gates, ms(gate), w_o, final_g, tm=ntok_s, final=last)
        for lst, val in zip(outs_s, (k, v, h_last, cbuf_t.transpose(1, 0, 2),
                                     s_re.reshape(nbs, S5_GROUPS, S5_STATE),
                                     s_im.reshape(nbs, S5_GROUPS, S5_STATE))):
            lst.append(val)

    k_p, v_p = (jnp.swapaxes(jnp.stack(o), 3, 4) for o in outs_p[:2])
    return (xp, xs.reshape(nbs, nts, D_MODEL), k_p, v_p,
            *[jnp.stack(o) for o in outs_p[2:]], *[jnp.stack(o) for o in outs_s])
```

```python
import functools

import jax
import jax.numpy as jnp
from jax import lax
from jax.experimental import pallas as pl
from jax.experimental.pallas import tpu as pltpu

F32 = jnp.float32
BF16 = jnp.bfloat16
HIGHEST = lax.Precision.HIGHEST

D_MODEL = 1024
DEPTH = 2
PAST_LEN = 16384
PAGE_SIZE = 128
D_LRU = 256
LRU_BLOCKS = 4
LRU_BW = D_LRU // LRU_BLOCKS
CONV_W = 4
LRU_C = 8.0
N_HEADS = 8
HEAD_DIM = 64
HALF = HEAD_DIM // 2
D_ATT = N_HEADS * HEAD_DIM
MOBA_BLOCK = 256
MOBA_TOPK = 3
ROPE_THETA = 10000.0
D_S5 = 256
S5_GROUP = 16
S5_GROUPS = D_S5 // S5_GROUP
S5_STATE = 64
S5_P = S5_GROUPS * S5_STATE
D_MIX = D_LRU + D_ATT + D_S5
EPS = 1e-6

LANES = 128
BF16_ROWS = 16
NEG = -1e30
LOG2E = 1.4426950408889634
VB_ROWS = HEAD_DIM + BF16_ROWS

U_COLS = D_LRU + D_S5
G_COLS = D_MIX
OFF_U = 0
OFF_G = OFF_U + U_COLS

VMEM_LIMIT = 56 * 1024 * 1024

NT_DIMS = (((1,), (1,)), ((), ()))


def _cparams(sem):
    return pltpu.CompilerParams(dimension_semantics=sem, vmem_limit_bytes=VMEM_LIMIT)


def _silu(x):
    return x * jax.nn.sigmoid(x)


def _mod_kernel(c_ref, w_ref, b_ref, o_ref):
    o_ref[0] = jnp.dot(_silu(c_ref[...]), w_ref[0], precision=HIGHEST,
                       preferred_element_type=F32) + b_ref[0]


def _modulation(c_all, w_ada, b_ada):
    n = c_all.shape[0]
    tn = D_MODEL
    return pl.pallas_call(
        _mod_kernel,
        grid=(DEPTH, 3 * D_MODEL // tn),
        in_specs=[pl.BlockSpec((n, D_MODEL), lambda l, j: (0, 0)),
                  pl.BlockSpec((1, D_MODEL, tn), lambda l, j: (l, 0, j)),
                  pl.BlockSpec((1, 1, tn), lambda l, j: (l, 0, j))],
        out_specs=pl.BlockSpec((1, n, tn), lambda l, j: (l, 0, j)),
        out_shape=jax.ShapeDtypeStruct((DEPTH, n, 3 * D_MODEL), F32),
        compiler_params=_cparams(("arbitrary", "arbitrary")),
        name="modulation",
    )(c_all, w_ada, b_ada.reshape(DEPTH, 1, 3 * D_MODEL))


def _inproj_kernel(x_ref, sc_ref, sh_ref, g_ref, w_ref, wt_ref, cost_ref, sint_ref,
                   u_ref, gate_ref, krm_ref, qt_ref, kt_ref, vt_ref):
    x = x_ref[0]
    h = x * lax.rsqrt(jnp.mean(x * x, axis=-1, keepdims=True) + EPS) * g_ref[...]
    h = h * (1.0 + sc_ref[0]) + sh_ref[0]
    hb = h.astype(BF16)

    def proj(lo, n):
        return jnp.dot(hb, w_ref[:, lo:lo + n], preferred_element_type=F32)

    u_ref[0] = proj(OFF_U, U_COLS)
    gate_ref[0] = _silu(proj(OFF_G, G_COLS)).astype(gate_ref.dtype)

    cost = cost_ref[...]
    sint = sint_ref[...]
    for n, (ref, use_rope) in enumerate(((qt_ref, True), (kt_ref, True), (vt_ref, False))):
        pt = lax.dot_general(wt_ref[n * D_ATT:(n + 1) * D_ATT, :], hb, NT_DIMS,
                             preferred_element_type=F32)
        if not use_rope:
            ref[0] = pt
            continue
        for hh in range(N_HEADS):
            r0 = hh * HEAD_DIM
            x1 = pt[r0:r0 + HALF]
            x2 = pt[r0 + HALF:r0 + HEAD_DIM]
            ref[0, r0:r0 + HALF, :] = x1 * cost - x2 * sint
            ref[0, r0 + HALF:r0 + HEAD_DIM, :] = x2 * cost + x1 * sint
    krm_ref[0] = jnp.transpose(kt_ref[0]).astype(BF16)


def _inproj(x, scale, shift, g, w_row, w_t, rope, tm):
    nb, s, _ = x.shape
    cos_t, sin_t = rope
    rm = scale.shape[1]
    mod_block = (1, tm, D_MODEL) if rm == s else (1, 1, D_MODEL)
    mod_map = (lambda b, i: (b, i, 0)) if rm == s else (lambda b, i: (b, 0, 0))
    fm_spec = pl.BlockSpec((1, D_ATT, tm), lambda b, i: (b, 0, i))
    fm_shape = jax.ShapeDtypeStruct((nb, D_ATT, s), F32)
    return pl.pallas_call(
        _inproj_kernel,
        grid=(nb, s // tm),
        in_specs=[pl.BlockSpec((1, tm, D_MODEL), lambda b, i: (b, i, 0)),
                  pl.BlockSpec(mod_block, mod_map),
                  pl.BlockSpec(mod_block, mod_map),
                  pl.BlockSpec((1, D_MODEL), lambda b, i: (0, 0)),
                  pl.BlockSpec(w_row.shape, lambda b, i: (0, 0)),
                  pl.BlockSpec(w_t.shape, lambda b, i: (0, 0)),
                  pl.BlockSpec((HALF, tm), lambda b, i: (0, i)),
                  pl.BlockSpec((HALF, tm), lambda b, i: (0, i))],
        out_specs=[pl.BlockSpec((1, tm, U_COLS), lambda b, i: (b, i, 0)),
                   pl.BlockSpec((1, tm, G_COLS), lambda b, i: (b, i, 0)),
                   pl.BlockSpec((1, tm, D_ATT), lambda b, i: (b, i, 0)),
                   fm_spec, fm_spec, fm_spec],
        out_shape=[jax.ShapeDtypeStruct((nb, s, U_COLS), F32),
                   jax.ShapeDtypeStruct((nb, s, G_COLS), BF16),
                   jax.ShapeDtypeStruct((nb, s, D_ATT), BF16),
                   fm_shape, fm_shape, fm_shape],
        compiler_params=_cparams(("arbitrary", "arbitrary")),
        name="inproj",
    )(x, scale, shift, g.reshape(1, D_MODEL), w_row, w_t, cos_t, sin_t)


def _outproj_kernel(x_ref, as_ref, att_ref, gate_ref, gm_ref, w_ref, fg_ref, o_ref, *, final):
    g = gate_ref[0].astype(F32)
    a_s = as_ref[0].astype(F32)
    mixed = jnp.concatenate([a_s[:, :D_LRU] * g[:, :D_LRU],
                             att_ref[0].astype(F32) * g[:, D_LRU:D_LRU + D_ATT],
                             a_s[:, D_LRU:] * g[:, D_LRU + D_ATT:]], axis=1).astype(BF16)
    y = jnp.dot(mixed, w_ref[...], preferred_element_type=F32)
    xn = x_ref[0] + gm_ref[0] * y
    if final:
        xn = xn * lax.rsqrt(jnp.mean(xn * xn, axis=-1, keepdims=True) + EPS) * fg_ref[...]
    o_ref[0] = xn


def _outproj(x, as_out, att, gates, gate_mod, w_out, final_g, tm, final):
    nb, s, _ = x.shape
    rm = gate_mod.shape[1]
    mod_block = (1, tm, D_MODEL) if rm == s else (1, 1, D_MODEL)
    mod_map = (lambda b, i: (b, i, 0)) if rm == s else (lambda b, i: (b, 0, 0))
    row = lambda n: pl.BlockSpec((1, tm, n), lambda b, i: (b, i, 0))
    return pl.pallas_call(
        functools.partial(_outproj_kernel, final=final),
        grid=(nb, s // tm),
        in_specs=[row(D_MODEL), row(U_COLS), row(D_ATT), row(G_COLS),
                  pl.BlockSpec(mod_block, mod_map),
                  pl.BlockSpec((D_MIX, D_MODEL), lambda b, i: (0, 0)),
                  pl.BlockSpec((1, D_MODEL), lambda b, i: (0, 0))],
        out_specs=row(D_MODEL),
        out_shape=jax.ShapeDtypeStruct((nb, s, D_MODEL), F32),
        compiler_params=_cparams(("arbitrary", "arbitrary")),
        name="outproj",
    )(x, as_out, att, gates, gate_mod, w_out, final_g.reshape(1, D_MODEL))


def _lru_gates(xc, wg_ref, bg_ref, c8_ref):
    gates = jnp.dot(xc.astype(BF16), wg_ref[...], preferred_element_type=F32) + bg_ref[...]
    r = jax.nn.sigmoid(gates[:, :D_LRU])
    i = jax.nn.sigmoid(gates[:, D_LRU:])
    log_a = r * c8_ref[...]
    a = jnp.exp(log_a)
    mult = jnp.sqrt(1.0 - jnp.exp(2.0 * log_a))
    return a, mult * i * xc


def _s5_out(x_cat_bf16, sx, cbd_ref, d_ref, wglu_ref, bglu_ref):
    y = jnp.dot(x_cat_bf16, cbd_ref[...], preferred_element_type=F32) + d_ref[...] * sx
    z = jax.nn.gelu(y)
    return z * jax.nn.sigmoid(jnp.dot(z.astype(BF16), wglu_ref[...], preferred_element_type=F32)
                              + bglu_ref[...])


def _rec_prompt_kernel(u_ref, cw_ref, cb_ref, wg_ref, bg_ref, c8_ref, lbr_ref, lbi_ref,
                       bbd_ref, cbd_ref, d_ref, wglu_ref, bglu_ref,
                       o_ref, hl_ref, cbuf_ref, sre_ref, sim_ref,
                       ext, a_s, b_s, h_s, bu_s, x_s, h_st, x_st, *, nb, tc, pitch):
    c = pl.program_id(0)
    nl = D_LRU // LANES
    ns = S5_P // LANES

    @pl.when(c == 0)
    def _():
        ext[:, 0:8, :] = jnp.zeros((nb, 8, D_LRU), F32)
        h_st[...] = jnp.zeros(h_st.shape, F32)
        x_st[...] = jnp.zeros(x_st.shape, F32)

    for b in range(nb):
        r0 = b * pitch
        ext[b, 8:8 + tc, :] = u_ref[b, :, 0:D_LRU]
        xc = cb_ref[...]
        for j in range(CONV_W):
            xc = xc + ext[b, 8 - (CONV_W - 1) + j:8 - (CONV_W - 1) + j + tc, :] * cw_ref[j:j + 1, :]
        ext[b, 8 - (CONV_W - 1):8, :] = ext[b, 8 + tc - (CONV_W - 1):8 + tc, :]
        a, bb = _lru_gates(xc, wg_ref, bg_ref, c8_ref)
        for j in range(nl):
            a_s[j, r0:r0 + tc, :] = a[:, j * LANES:(j + 1) * LANES]
            b_s[j, r0:r0 + tc, :] = bb[:, j * LANES:(j + 1) * LANES]
        sxb = u_ref[b, :, D_LRU:U_COLS].astype(BF16)
        for j in range(ns):
            bu = jnp.dot(sxb, bbd_ref[:, 2 * j * LANES:(2 * j + 2) * LANES], preferred_element_type=F32)
            bu_s[j, r0:r0 + tc, :] = bu[:, :LANES]
            bu_s[ns + j, r0:r0 + tc, :] = bu[:, LANES:]

    def step(t, carry):
        h, xr, xi = carry
        rows = pl.ds(t, nb, stride=pitch)
        h_new, xr_new, xi_new = [], [], []
        for j in range(nl):
            hj = a_s[j, rows, :] * h[j] + b_s[j, rows, :]
            h_s[j, rows, :] = hj
            h_new.append(hj)
        for j in range(ns):
            lr = lbr_ref[j]
            li = lbi_ref[j]
            nr = lr * xr[j] - li * xi[j] + bu_s[j, rows, :]
            ni = lr * xi[j] + li * xr[j] + bu_s[ns + j, rows, :]
            x_s[j, rows, :] = nr
            x_s[ns + j, rows, :] = ni
            xr_new.append(nr)
            xi_new.append(ni)
        return tuple(h_new), tuple(xr_new), tuple(xi_new)

    init = (tuple(h_st[j] for j in range(nl)),
            tuple(x_st[j] for j in range(ns)),
            tuple(x_st[ns + j] for j in range(ns)))
    h, xr, xi = lax.fori_loop(0, tc, step, init)
    for j in range(nl):
        h_st[j] = h[j]
        hl_ref[:, j * LANES:(j + 1) * LANES] = h[j]
    for j in range(ns):
        x_st[j] = xr[j]
        x_st[ns + j] = xi[j]
        sre_ref[:, j * LANES:(j + 1) * LANES] = xr[j]
        sim_ref[:, j * LANES:(j + 1) * LANES] = xi[j]

    for b in range(nb):
        r0 = b * pitch
        for j in range(nl):
            o_ref[b, :, j * LANES:(j + 1) * LANES] = h_s[j, r0:r0 + tc, :].astype(o_ref.dtype)
        x_cat = jnp.concatenate([x_s[j, r0:r0 + tc, :].astype(BF16) for j in range(2 * ns)], axis=1)
        o_ref[b, :, D_LRU:U_COLS] = _s5_out(x_cat, u_ref[b, :, D_LRU:U_COLS], cbd_ref, d_ref,
                                            wglu_ref, bglu_ref).astype(o_ref.dtype)
        cbuf_ref[b] = ext[b, 8 - (CONV_W - 1):8, :]


def _rec_prompt(u, pr, tc):
    nb, s, _ = u.shape
    pitch = tc + 8
    nl = D_LRU // LANES
    ns = S5_P // LANES
    full = lambda a: pl.BlockSpec(a.shape, lambda c, _n=a.ndim: (0,) * _n)
    params = [pr["cw"], pr["cb"], pr["wg"], pr["bg"], pr["c8"], pr["lbr"], pr["lbi"],
              pr["bbd"], pr["cbd"], pr["d"], pr["wglu"], pr["bglu"]]
    st = lambda *shape: pl.BlockSpec(shape, lambda c, _n=len(shape): (0,) * _n)
    return pl.pallas_call(
        functools.partial(_rec_prompt_kernel, nb=nb, tc=tc, pitch=pitch),
        grid=(s // tc,),
        in_specs=[pl.BlockSpec((nb, tc, U_COLS), lambda c: (0, c, 0))] + [full(a) for a in params],
        out_specs=[pl.BlockSpec((nb, tc, U_COLS), lambda c: (0, c, 0)),
                   st(nb, D_LRU), st(nb, CONV_W - 1, D_LRU), st(nb, S5_P), st(nb, S5_P)],
        out_shape=[jax.ShapeDtypeStruct((nb, s, U_COLS), BF16),
                   jax.ShapeDtypeStruct((nb, D_LRU), F32),
                   jax.ShapeDtypeStruct((nb, CONV_W - 1, D_LRU), F32),
                   jax.ShapeDtypeStruct((nb, S5_P), F32),
                   jax.ShapeDtypeStruct((nb, S5_P), F32)],
        scratch_shapes=[pltpu.VMEM((nb, tc + 8, D_LRU), F32),
                        pltpu.VMEM((nl, nb * pitch, LANES), F32),
                        pltpu.VMEM((nl, nb * pitch, LANES), F32),
                        pltpu.VMEM((nl, nb * pitch, LANES), F32),
                        pltpu.VMEM((2 * ns, nb * pitch, LANES), F32),
                        pltpu.VMEM((2 * ns, nb * pitch, LANES), F32),
                        pltpu.VMEM((nl, nb, LANES), F32),
                        pltpu.VMEM((2 * ns, nb, LANES), F32)],
        compiler_params=_cparams(("arbitrary",)),
        name="rec_prompt",
    )(u, *params)


def _rec_sample_kernel(u_ref, h0_ref, buf0_ref, s0r_ref, s0i_ref,
                       cw_ref, cb_ref, wg_ref, bg_ref, c8_ref, lbr_ref, lbi_ref,
                       bbd_ref, cbd_ref, d_ref, wglu_ref, bglu_ref,
                       o_ref, hl_ref, cbuf_ref, sre_ref, sim_ref, *, nt):
    ns = S5_P // LANES
    ext = [buf0_ref[j] for j in range(CONV_W - 1)] + [u_ref[t, :, 0:D_LRU] for t in range(nt)]
    xcs = []
    for t in range(nt):
        xc = cb_ref[...]
        for j in range(CONV_W):
            xc = xc + ext[t + j] * cw_ref[j:j + 1, :]
        xcs.append(xc)
    a, bb = _lru_gates(jnp.concatenate(xcs, axis=0), wg_ref, bg_ref, c8_ref)
    nbat = h0_ref.shape[0]
    h = h0_ref[...]
    for t in range(nt):
        h = a[t * nbat:(t + 1) * nbat] * h + bb[t * nbat:(t + 1) * nbat]
        o_ref[t, :, 0:D_LRU] = h.astype(o_ref.dtype)
    hl_ref[...] = h
    for j in range(CONV_W - 1):
        cbuf_ref[j] = ext[nt + j]

    sx = jnp.concatenate([u_ref[t, :, D_LRU:U_COLS] for t in range(nt)], axis=0)
    bu = jnp.dot(sx.astype(BF16), bbd_ref[...], preferred_element_type=F32)
    lr = jnp.concatenate([lbr_ref[j] for j in range(ns)], axis=1)
    li = jnp.concatenate([lbi_ref[j] for j in range(ns)], axis=1)
    xr = s0r_ref[...]
    xi = s0i_ref[...]
    xs = []
    for t in range(nt):
        but = bu[t * nbat:(t + 1) * nbat]
        bur = jnp.concatenate([but[:, 2 * j * LANES:(2 * j + 1) * LANES] for j in range(ns)], axis=1)
        bui = jnp.concatenate([but[:, (2 * j + 1) * LANES:(2 * j + 2) * LANES] for j in range(ns)], axis=1)
        xr, xi = lr * xr - li * xi + bur, lr * xi + li * xr + bui
        xs.append(jnp.concatenate([xr, xi], axis=1).astype(BF16))
    sre_ref[...] = xr
    sim_ref[...] = xi
    s_out = _s5_out(jnp.concatenate(xs, axis=0), sx, cbd_ref, d_ref, wglu_ref, bglu_ref)
    for t in range(nt):
        o_ref[t, :, D_LRU:U_COLS] = s_out[t * nbat:(t + 1) * nbat].astype(o_ref.dtype)


def _rec_sample(u_t, h0, buf0_t, s0r, s0i, pr):
    nt, nbat, _ = u_t.shape
    params = [pr["cw"], pr["cb"], pr["wg"], pr["bg"], pr["c8"], pr["lbr"], pr["lbi"],
              pr["bbd"], pr["cbd"], pr["d"], pr["wglu"], pr["bglu"]]
    return pl.pallas_call(
        functools.partial(_rec_sample_kernel, nt=nt),
        out_shape=[jax.ShapeDtypeStruct((nt, nbat, U_COLS), BF16),
                   jax.ShapeDtypeStruct((nbat, D_LRU), F32),
                   jax.ShapeDtypeStruct((CONV_W - 1, nbat, D_LRU), F32),
                   jax.ShapeDtypeStruct((nbat, S5_P), F32),
                   jax.ShapeDtypeStruct((nbat, S5_P), F32)],
        compiler_params=pltpu.CompilerParams(vmem_limit_bytes=VMEM_LIMIT),
        name="rec_sample",
    )(u_t, h0, buf0_t, s0r, s0i, *params)


def _rec_params(lru_conv_w, lru_conv_b, lru_w_r, lru_b_r, lru_w_i, lru_b_i, lru_lam,
                s5_a_re, s5_a_im, s5_log_dt, s5_b_re, s5_b_im, s5_c_re, s5_c_im, s5_d, s5_w_glu, s5_b_glu):
    ns = S5_P // LANES
    def bd(w):
        n, r, c = w.shape
        on_diag = jnp.eye(n, dtype=bool)[:, None, :, None]
        return jnp.where(on_diag, w[:, :, None, :], 0.0).reshape(n * r, n * c)
    wg = jnp.concatenate([bd(lru_w_r), bd(lru_w_i)], axis=1).astype(BF16)
    bg = jnp.concatenate([lru_b_r, lru_b_i]).reshape(1, 2 * D_LRU)
    c8 = (LRU_C * jax.nn.log_sigmoid(lru_lam.astype(F32))).reshape(1, D_LRU)
    dt = jnp.exp(s5_log_dt.astype(F32))[:, None]
    ar = s5_a_re.astype(F32)
    ai = s5_a_im.astype(F32)
    mag = jnp.exp(ar * dt)
    lb_re = mag * jnp.cos(ai * dt)
    lb_im = mag * jnp.sin(ai * dt)
    den = ar * ar + ai * ai
    n_re = lb_re - 1.0
    co_re = (n_re * ar + lb_im * ai) / den
    co_im = (lb_im * ar - n_re * ai) / den
    bb_re = co_re[..., None] * s5_b_re - co_im[..., None] * s5_b_im
    bb_im = co_re[..., None] * s5_b_im + co_im[..., None] * s5_b_re
    bre = bd(jnp.swapaxes(bb_re, 1, 2))
    bim = bd(jnp.swapaxes(bb_im, 1, 2))
    bbd = jnp.stack([bre.reshape(D_S5, ns, LANES), bim.reshape(D_S5, ns, LANES)], axis=2)
    bbd = bbd.reshape(D_S5, 2 * S5_P).astype(BF16)
    cre = bd(jnp.swapaxes(s5_c_re, 1, 2))
    cim = bd(jnp.swapaxes(s5_c_im, 1, 2))
    cbd = jnp.concatenate([cre, -cim], axis=0).astype(BF16)
    return dict(cw=lru_conv_w, cb=lru_conv_b.reshape(1, D_LRU), wg=wg, bg=bg, c8=c8,
                lbr=lb_re.reshape(ns, 1, LANES), lbi=lb_im.reshape(ns, 1, LANES),
                bbd=bbd, cbd=cbd, d=s5_d.reshape(1, D_S5),
                wglu=s5_w_glu.astype(BF16), bglu=s5_b_glu.reshape(1, D_S5))


def _moba_prompt_kernel(pt_ref, qt_ref, krm_ref, kt_ref, vt_ref, cache_ref, o_ref, ks_ref,
                        km_s, vb_s, bias_s, sc_s, pbuf, psem, *, s, nblk, grp, nh, layer, pg, nsteps):
    qi = pl.program_id(2)
    step = (pl.program_id(0) * (N_HEADS // nh) + pl.program_id(1)) * nblk + qi
    tq = MOBA_BLOCK
    hd = HEAD_DIM
    gk = grp * tq

    @pl.when(qi == 0)
    def _():
        r = lax.broadcasted_iota(jnp.int32, (nblk, s), 1) // MOBA_BLOCK
        n = lax.broadcasted_iota(jnp.int32, (nblk, s), 0)
        pm = jnp.where(r == n, 1.0 / MOBA_BLOCK, 0.0).astype(F32)
        for hh in range(nh):
            km_s[hh] = lax.dot_general(pm, kt_ref[0, hh * hd:(hh + 1) * hd, :], NT_DIMS,
                                       precision=HIGHEST, preferred_element_type=F32)
        ones_row = (lax.broadcasted_iota(jnp.int32, (VB_ROWS - hd, gk), 0) == 0).astype(BF16)
        for g in range(nblk // grp):
            for hh in range(nh):
                vb_s[g, hh, 0:hd, :] = vt_ref[0, hh * hd:(hh + 1) * hd, g * gk:(g + 1) * gk].astype(BF16)
                vb_s[g, hh, hd:VB_ROWS, :] = ones_row

    qt = qt_ref[0]
    qs = (qt * (hd ** -0.5 * LOG2E)).astype(BF16)
    feat = lax.broadcasted_iota(jnp.int32, (nh * hd, tq), 0)
    blk_iota = lax.broadcasted_iota(jnp.int32, (nblk, tq), 0)
    past = blk_iota < qi
    ws = []
    for hh in range(nh):
        ws.append(jnp.where((feat >= hh * hd) & (feat < (hh + 1) * hd), qs, jnp.zeros_like(qs)))
        sb = jnp.dot(km_s[hh], qt[hh * hd:(hh + 1) * hd, :], precision=HIGHEST,
                     preferred_element_type=F32)
        sb = jnp.where(past, sb, -jnp.inf)
        cnt = jnp.zeros((nblk, tq), jnp.int32)
        for i in range(nblk):
            ri = sb[i:i + 1, :]
            beats = (ri > sb) | ((ri == sb) & (i < blk_iota))
            cnt = cnt + beats.astype(jnp.int32)
        sel = (cnt < MOBA_TOPK) & past
        bias_s[hh] = jnp.where(sel | (blk_iota == qi), 0.0, NEG).astype(F32)

    def scores(g, hh):
        st = pl.multiple_of(g * gk, gk)
        sc = jnp.dot(krm_ref[0, pl.ds(st, gk), :], ws[hh], preferred_element_type=F32)
        return jnp.concatenate(
            [sc[i * tq:(i + 1) * tq] + bias_s[hh, pl.ds(g * grp + i, 1), :] for i in range(grp)], axis=0)

    def pass1(g, ms):
        st = pl.multiple_of(g * gk, gk)
        out = []
        for hh in range(nh):
            sc = scores(g, hh)
            sc_s[hh, pl.ds(st, gk), :] = sc
            out.append(jnp.maximum(ms[hh], jnp.max(sc, axis=0, keepdims=True)))
        return tuple(out)

    g_own = qi // grp
    ms = lax.fori_loop(0, g_own, pass1, tuple(jnp.full((1, tq), NEG, F32) for _ in range(nh)))

    _ksum_fetch(pt_ref, cache_ref, ks_ref, pbuf, psem, step, layer=layer, nsteps=nsteps, pg=pg)
    half = N_HEADS // 2

    r_own = qi % grp
    st_own = pl.multiple_of(g_own * gk, gk)
    causal = (lax.broadcasted_iota(jnp.int32, (tq, tq), 0) <= lax.broadcasted_iota(jnp.int32, (tq, tq), 1))

    def own_pass1(nb_own):
        rows = nb_own * tq

        def run(ms):
            _ksum_merge(ks_ref, pbuf, step, range(0, half), pg=pg)
            out = []
            for hh in range(nh):
                sc = jnp.dot(krm_ref[0, pl.ds(st_own, rows), :], ws[hh], preferred_element_type=F32)
                parts = [sc[i * tq:(i + 1) * tq] + bias_s[hh, pl.ds(g_own * grp + i, 1), :]
                         for i in range(nb_own - 1)]
                parts.append(jnp.where(causal, sc[rows - tq:rows], NEG))
                sc = jnp.concatenate(parts, axis=0)
                sc_s[hh, pl.ds(st_own, rows), :] = sc
                out.append(jnp.maximum(ms[hh], jnp.max(sc, axis=0, keepdims=True)))
            return tuple(out)
        return run

    ms = lax.switch(r_own, [own_pass1(i + 1) for i in range(grp)], ms)

    def pass2_group(g, rows, accs):
        st = pl.multiple_of(g * gk, gk)
        out = []
        for hh in range(nh):
            p = jnp.exp2(sc_s[hh, pl.ds(st, rows), :] - ms[hh])
            out.append(accs[hh] + jnp.dot(vb_s[g, hh, :, 0:rows], p.astype(BF16), preferred_element_type=F32))
        return tuple(out)

    accs = lax.fori_loop(0, g_own, lambda g, a: pass2_group(g, gk, a),
                         tuple(jnp.zeros((VB_ROWS, tq), F32) for _ in range(nh)))
    def own_pass2(nb_own):
        def run(accs):
            _ksum_merge(ks_ref, pbuf, step, range(half, N_HEADS), pg=pg)
            return pass2_group(g_own, nb_own * tq, accs)
        return run

    accs = lax.switch(r_own, [own_pass2(i + 1) for i in range(grp)], accs)
    out_t = jnp.concatenate([acc[0:hd] / acc[hd:hd + 1] for acc in accs], axis=0)
    o_ref[0] = jnp.transpose(out_t).astype(o_ref.dtype)


def _moba_prompt(qt, krm, kt, vt, pt_flat, cache_kt, layer):
    nb, _, s = qt.shape
    nblk = s // MOBA_BLOCK
    tq = MOBA_BLOCK
    nh = 4
    grp = 4
    hp = nh * HEAD_DIM
    nhp = N_HEADS // nh
    nsteps = nb * nhp * nblk
    pg = pt_flat.shape[0] // nsteps
    nbat = pt_flat.shape[0] // N_PAGES
    steps_per_b = N_PAGES // pg
    once = dict(pipeline_mode=pl.Buffered(1))
    return pl.pallas_call(
        functools.partial(_moba_prompt_kernel, s=s, nblk=nblk, grp=grp, nh=nh, layer=layer, pg=pg,
                          nsteps=nsteps),
        grid_spec=pltpu.PrefetchScalarGridSpec(
            num_scalar_prefetch=1,
            grid=(nb, nhp, nblk),
            in_specs=[pl.BlockSpec((1, hp, tq), lambda b, h, i, pt: (b, h, i)),
                      pl.BlockSpec((1, s, hp), lambda b, h, i, pt: (b, 0, h), **once),
                      pl.BlockSpec((1, hp, s), lambda b, h, i, pt: (b, h, 0), **once),
                      pl.BlockSpec((1, hp, s), lambda b, h, i, pt: (b, h, 0), **once),
                      pl.BlockSpec(memory_space=pl.ANY)],
            out_specs=[pl.BlockSpec((1, tq, hp), lambda b, h, i, pt: (b, i, h)),
                       pl.BlockSpec((1, N_HEADS, HEAD_DIM, LANES),
                                    lambda b, h, i, pt: (((b * nhp + h) * nblk + i) // steps_per_b, 0, 0, 0))],
            scratch_shapes=[pltpu.VMEM((nh, nblk, HEAD_DIM), F32),
                            pltpu.VMEM((nblk // grp, nh, VB_ROWS, grp * tq), BF16),
                            pltpu.VMEM((nh, nblk, tq), F32),
                            pltpu.VMEM((nh, s, tq), F32),
                            pltpu.VMEM((2, pg, N_HEADS, HEAD_DIM, PAGE_SIZE), F32),
                            pltpu.SemaphoreType.DMA((2, pg))]),
        out_shape=[jax.ShapeDtypeStruct((nb, s, D_ATT), BF16),
                   jax.ShapeDtypeStruct((nbat, N_HEADS, HEAD_DIM, LANES), F32)],
        compiler_params=_cparams(("arbitrary", "arbitrary", "arbitrary")),
        name="moba_prompt",
    )(pt_flat, qt, krm, kt, vt, cache_kt)


PAGES_PER_BLOCK = MOBA_BLOCK // PAGE_SIZE
N_PAGES = PAST_LEN // PAGE_SIZE
N_FULL = PAST_LEN // MOBA_BLOCK
SAMPLE_SLOTS = 3


def _ksum_fetch(pt_ref, cache_ref, o_ref, buf, sem, n, *, layer, nsteps, pg):
    slot = n % 2
    g = n % (N_PAGES // pg)

    def start(step, sl):
        for j in range(pg):
            pid = pt_ref[step * pg + j]
            pltpu.make_async_copy(cache_ref.at[layer, pid], buf.at[sl, j], sem.at[sl, j]).start()

    @pl.when(n == 0)
    def _():
        start(0, 0)

    @pl.when(n + 1 < nsteps)
    def _():
        start(n + 1, 1 - slot)

    @pl.when(g == 0)
    def _():
        o_ref[...] = jnp.zeros(o_ref.shape, F32)

    for j in range(pg):
        pltpu.make_async_copy(buf.at[slot, j], buf.at[slot, j], sem.at[slot, j]).wait()


def _ksum_merge(o_ref, buf, n, heads, *, pg):
    slot = n % 2
    g = n % (N_PAGES // pg)
    bpg = pg // PAGES_PER_BLOCK
    lane = lax.broadcasted_iota(jnp.int32, (HEAD_DIM, LANES), 1)
    for hh in heads:
        acc = o_ref[0, hh]
        for blk in range(bpg):
            tile = buf[slot, PAGES_PER_BLOCK * blk, hh]
            for pp in range(1, PAGES_PER_BLOCK):
                tile = tile + buf[slot, PAGES_PER_BLOCK * blk + pp, hh]
            acc = jnp.where(lane == g * bpg + blk, jnp.sum(tile, axis=1, keepdims=True), acc)
        o_ref[0, hh] = acc


def _topk_kernel(q_ref, ks_ref, o_ref):
    nq = q_ref.shape[2]
    lane = lax.broadcasted_iota(jnp.int32, (nq, LANES), 1)
    for hh in range(N_HEADS):
        km = ks_ref[0, hh] * (1.0 / MOBA_BLOCK)
        sb = jnp.dot(q_ref[0, hh], km, precision=HIGHEST, preferred_element_type=F32)
        sb = jnp.where(lane < N_FULL, sb, -jnp.inf)
        out = jnp.zeros((nq, LANES), jnp.int32)
        for r in range(MOBA_TOPK):
            mx = jnp.max(sb, axis=1, keepdims=True)
            idx = jnp.min(jnp.where(sb == mx, lane, LANES), axis=1, keepdims=True)
            out = jnp.where(lane == r, idx, out)
            sb = jnp.where(lane == idx, -jnp.inf, sb)
        o_ref[0, hh] = out


def _sample_topk(q_pad, ksum_t):
    nbat, nh, nq, hd = q_pad.shape
    return pl.pallas_call(
        _topk_kernel,
        grid=(nbat,),
        in_specs=[pl.BlockSpec((1, nh, nq, hd), lambda b: (b, 0, 0, 0)),
                  pl.BlockSpec((1, nh, hd, LANES), lambda b: (b, 0, 0, 0))],
        out_specs=pl.BlockSpec((1, nh, nq, LANES), lambda b: (b, 0, 0, 0)),
        out_shape=jax.ShapeDtypeStruct((nbat, nh, nq, LANES), jnp.int32),
        compiler_params=_cparams(("arbitrary",)),
        name="sample_topk",
    )(q_pad, ksum_t)


def _slab_copy(src, buf, sem, slot, j):
    return pltpu.make_async_copy(src, buf.at[slot, j], sem.at[slot, j])


def _moba_sample_kernel(idx_ref, pt_ref, q_ref, kn_ref, vn_ref, ck_ref, cv_ref, o_ref,
                        kbuf, vbuf, ksem, vsem, *, layer, nsteps, nt, hps):
    n = pl.program_id(0)
    slot = n % SAMPLE_SLOTS
    b = n // (N_HEADS // hps)
    nsel = MOBA_TOPK * PAGES_PER_BLOCK
    nslab = nt * nsel
    ahead = SAMPLE_SLOTS - 1

    def start(step, sl):
        sb = step // (N_HEADS // hps)
        for e in range(hps):
            bh = step * hps + e
            sh = bh % N_HEADS
            for t in range(nt):
                for r in range(MOBA_TOPK):
                    blk = idx_ref[(bh * nt + t) * MOBA_TOPK + r]
                    for pp in range(PAGES_PER_BLOCK):
                        pid = pt_ref[sb * N_PAGES + blk * PAGES_PER_BLOCK + pp]
                        j = e * nslab + (t * MOBA_TOPK + r) * PAGES_PER_BLOCK + pp
                        _slab_copy(ck_ref.at[layer, pid, sh], kbuf, ksem, sl, j).start()
                        _slab_copy(cv_ref.at[layer, pid, sh], vbuf, vsem, sl, j).start()

    def wait(sl):
        for j in range(hps * nslab):
            _slab_copy(kbuf.at[sl, j], kbuf, ksem, sl, j).wait()
            _slab_copy(vbuf.at[sl, j], vbuf, vsem, sl, j).wait()

    @pl.when(n == 0)
    def _():
        for s0 in range(ahead):
            start(s0, s0)

    @pl.when(n + ahead < nsteps)
    def _():
        start(n + ahead, (n + ahead) % SAMPLE_SLOTS)

    wait(slot)

    nq = q_ref.shape[2]
    ntok = kn_ref.shape[2]
    nkeys = nsel * PAGE_SIZE
    qrow = lax.broadcasted_iota(jnp.int32, (nq, nkeys), 0)
    orow = lax.broadcasted_iota(jnp.int32, (nq, ntok), 0)
    opos = lax.broadcasted_iota(jnp.int32, (nq, ntok), 1) - b * nt
    own_ok = (opos >= 0) & (opos <= orow)
    for e in range(hps):
        qs = (q_ref[0, e] * (HEAD_DIM ** -0.5)).astype(BF16)
        base = e * nslab
        scores = []
        for t in range(nt):
            kt = jnp.concatenate([kbuf[slot, base + t * nsel + i] for i in range(nsel)], axis=1).astype(BF16)
            st = jnp.dot(qs, kt, preferred_element_type=F32)
            scores.append(jnp.where(qrow == t, st, -jnp.inf))
        so = jnp.dot(qs, kn_ref[e].astype(BF16), preferred_element_type=F32)
        so = jnp.where(own_ok, so, -jnp.inf)
        m = jnp.max(so, axis=1, keepdims=True)
        for st in scores:
            m = jnp.maximum(m, jnp.max(st, axis=1, keepdims=True))
        po = jnp.exp(so - m)
        den = jnp.sum(po, axis=1, keepdims=True)
        acc = lax.dot_general(po.astype(BF16), vn_ref[e].astype(BF16), NT_DIMS, preferred_element_type=F32)
        for t in range(nt):
            pt_ = jnp.exp(scores[t] - m)
            den = den + jnp.sum(pt_, axis=1, keepdims=True)
            vt = jnp.concatenate([vbuf[slot, base + t * nsel + i] for i in range(nsel)], axis=1).astype(BF16)
            acc = acc + lax.dot_general(pt_.astype(BF16), vt, NT_DIMS, preferred_element_type=F32)
        o_ref[0, e] = acc / den


def _moba_sample(q_pad, kt_new, vt_new, cache_kt, cache_vt, idx_flat, pt_flat, layer, nt):
    nbat, nh, nq, hd = q_pad.shape
    ntok = kt_new.shape[1]
    hps = 2
    npair = nh // hps
    nsteps = nbat * npair
    nslab = hps * nt * MOBA_TOPK * PAGES_PER_BLOCK
    qtok = pl.BlockSpec((1, hps, nq, hd), lambda n, idx, pt: (n // npair, n % npair, 0, 0))
    new = pl.BlockSpec((hps, hd, ntok), lambda n, idx, pt: (n % npair, 0, 0))
    return pl.pallas_call(
        functools.partial(_moba_sample_kernel, layer=layer, nsteps=nsteps, nt=nt, hps=hps),
        grid_spec=pltpu.PrefetchScalarGridSpec(
            num_scalar_prefetch=2,
            grid=(nsteps,),
            in_specs=[qtok, new, new, pl.BlockSpec(memory_space=pl.ANY), pl.BlockSpec(memory_space=pl.ANY)],
            out_specs=qtok,
            scratch_shapes=[pltpu.VMEM((SAMPLE_SLOTS, nslab, hd, PAGE_SIZE), F32),
                            pltpu.VMEM((SAMPLE_SLOTS, nslab, hd, PAGE_SIZE), F32),
                            pltpu.SemaphoreType.DMA((SAMPLE_SLOTS, nslab)),
                            pltpu.SemaphoreType.DMA((SAMPLE_SLOTS, nslab))]),
        out_shape=jax.ShapeDtypeStruct((nbat, nh, nq, hd), F32),
        compiler_params=_cparams(("arbitrary",)),
        name="moba_sample",
    )(idx_flat, pt_flat, q_pad, kt_new.reshape(nh, hd, ntok), vt_new.reshape(nh, hd, ntok),
      cache_kt, cache_vt)


def _rope_tables(pos):
    inv = jnp.power(ROPE_THETA, -jnp.arange(HALF, dtype=F32) / HALF)
    ang = inv[:, None] * pos.astype(F32)[None, :]
    return jnp.cos(ang), jnp.sin(ang)


def _split_w_in(w):
    a_x, a_g, q, k, v, b_g, s_x, s_g = jnp.split(
        w, (D_LRU, 2 * D_LRU, 2 * D_LRU + D_ATT, 2 * D_LRU + 2 * D_ATT, 2 * D_LRU + 3 * D_ATT,
            2 * D_LRU + 4 * D_ATT, 2 * D_LRU + 4 * D_ATT + D_S5), axis=1)
    w_row = jnp.concatenate([a_x, s_x, a_g, b_g, s_g], axis=1).astype(BF16)
    w_t = jnp.concatenate([q, k, v], axis=1).T.astype(BF16)
    return w_row, w_t


def kernel(x_prompt, x_sample, cache_k, cache_v, state_lru_h, state_lru_conv, state_s5_re, state_s5_im,
           page_table, c_prompt, c_sample, norm_g, w_ada, b_ada, w_in, w_out, lru_conv_w, lru_conv_b,
           lru_w_r, lru_b_r, lru_w_i, lru_b_i, lru_lam, s5_a_re, s5_a_im, s5_log_dt, s5_b_re, s5_b_im,
           s5_c_re, s5_c_im, s5_d, s5_w_glu, s5_b_glu, final_g):
    nbp, seq, _ = x_prompt.shape
    nbs, nts, _ = x_sample.shape
    ntok_s = nbs * nts

    rope_p = _rope_tables(jnp.arange(seq, dtype=jnp.int32))
    rope_s = _rope_tables(PAST_LEN + (jnp.arange(ntok_s, dtype=jnp.int32) % nts))

    n_c = nbp + nbs
    c_all = jnp.concatenate([c_prompt, c_sample, jnp.zeros((-n_c % 8, D_MODEL), F32)], axis=0)
    mod = _modulation(c_all, w_ada, b_ada)

    cache_kt = jnp.swapaxes(cache_k, 3, 4)
    cache_vt = jnp.swapaxes(cache_v, 3, 4)
    pt_flat = page_table.reshape(-1)
    xp = x_prompt
    xs = x_sample.reshape(1, ntok_s, D_MODEL)
    outs_p = [[] for _ in range(6)]
    outs_s = [[] for _ in range(6)]
    for l in range(DEPTH):
        last = l == DEPTH - 1
        w_row, w_t = _split_w_in(w_in[l])
        w_o = w_out[l].astype(BF16)
        pr = _rec_params(lru_conv_w[l], lru_conv_b[l], lru_w_r[l], lru_b_r[l], lru_w_i[l], lru_b_i[l],
                         lru_lam[l], s5_a_re[l], s5_a_im[l], s5_log_dt[l], s5_b_re[l], s5_b_im[l],
                         s5_c_re[l], s5_c_im[l], s5_d[l], s5_w_glu[l], s5_b_glu[l])
        shift, scale, gate = jnp.split(mod[l], 3, axis=-1)

        mp = lambda a: a[:nbp].reshape(nbp, 1, D_MODEL)
        u, gates, krm, qt, kt, vt = _inproj(xp, mp(scale), mp(shift), norm_g[l], w_row, w_t, rope_p, tm=512)
        as_out, h_last, cbuf, s_re, s_im = _rec_prompt(u, pr, tc=256)
        att, ksum_t = _moba_prompt(qt, krm, kt, vt, pt_flat, cache_kt, l)
        xp = _outproj(xp, as_out, att, gates, mp(gate), w_o, final_g, tm=512, final=last)
        for lst, val in zip(outs_p, (kt.reshape(nbp, N_HEADS, HEAD_DIM, seq), vt.reshape(nbp, N_HEADS, HEAD_DIM, seq),
                                     h_last, cbuf, s_re.reshape(nbp, S5_GROUPS, S5_STATE),
                                     s_im.reshape(nbp, S5_GROUPS, S5_STATE))):
            lst.append(val)

        ms = lambda a: jnp.repeat(a[nbp:n_c], nts, axis=0).reshape(1, ntok_s, D_MODEL)
        u, gates, _, qt, kt, vt = _inproj(xs, ms(scale), ms(shift), norm_g[l], w_row, w_t, rope_s, tm=ntok_s)
        to_heads = lambda a: a.reshape(N_HEADS, HEAD_DIM, nbs, nts).transpose(2, 0, 3, 1)
        q, k, v = to_heads(qt), to_heads(kt), to_heads(vt)
        u_t = u.reshape(nbs, nts, U_COLS).transpose(1, 0, 2)
        as_t, h_last, cbuf_t, s_re, s_im = _rec_sample(
            u_t, state_lru_h[l], state_lru_conv[l].transpose(1, 0, 2),
            state_s5_re[l].reshape(nbs, S5_P), state_s5_im[l].reshape(nbs, S5_P), pr)
        as_out = as_t.transpose(1, 0, 2).reshape(1, ntok_s, U_COLS)
        q_pad = jnp.pad(q, ((0, 0), (0, 0), (0, -nts % 8), (0, 0)))
        idx = _sample_topk(q_pad, ksum_t)[:, :, :nts, :MOBA_TOPK].reshape(-1)
        att = _moba_sample(q_pad, kt[0], vt[0], cache_kt, cache_vt, idx, pt_flat, l, nts)[:, :, :nts]
        att = att.transpose(0, 2, 1, 3).reshape(1, ntok_s, D_ATT).astype(BF16)
        xs = _outproj(xs, as_out, att, gates, ms(gate), w_o, final_g, tm=ntok_s, final=last)
        for lst, val in zip(outs_s, (k, v, h_last, cbuf_t.transpose(1, 0, 2),
                                     s_re.reshape(nbs, S5_GROUPS, S5_STATE),
                                     s_im.reshape(nbs, S5_GROUPS, S5_STATE))):
            lst.append(val)

    k_p, v_p = (jnp.swapaxes(jnp.stack(o), 3, 4) for o in outs_p[:2])
    return (xp, xs.reshape(nbs, nts, D_MODEL), k_p, v_p,
            *[jnp.stack(o) for o in outs_p[2:]], *[jnp.stack(o) for o in outs_s])
```

```python
import functools

import jax
import jax.numpy as jnp
from jax import lax
from jax.experimental import pallas as pl
from jax.experimental.pallas import tpu as pltpu

F32 = jnp.float32
BF16 = jnp.bfloat16
HIGHEST = lax.Precision.HIGHEST

D_MODEL = 1024
DEPTH = 2
PAST_LEN = 16384
PAGE_SIZE = 128
D_LRU = 256
LRU_BLOCKS = 4
LRU_BW = D_LRU // LRU_BLOCKS
CONV_W = 4
LRU_C = 8.0
N_HEADS = 8
HEAD_DIM = 64
HALF = HEAD_DIM // 2
D_ATT = N_HEADS * HEAD_DIM
MOBA_BLOCK = 256
MOBA_TOPK = 3
ROPE_THETA = 10000.0
D_S5 = 256
S5_GROUP = 16
S5_GROUPS = D_S5 // S5_GROUP
S5_STATE = 64
S5_P = S5_GROUPS * S5_STATE
D_MIX = D_LRU + D_ATT + D_S5
EPS = 1e-6

LANES = 128
BF16_ROWS = 16
NEG = -1e30
LOG2E = 1.4426950408889634
VB_ROWS = HEAD_DIM + BF16_ROWS

U_COLS = D_LRU + D_S5
G_COLS = D_MIX
OFF_U = 0
OFF_G = OFF_U + U_COLS

VMEM_LIMIT = 56 * 1024 * 1024

NT_DIMS = (((1,), (1,)), ((), ()))


def _cparams(sem):
    return pltpu.CompilerParams(dimension_semantics=sem, vmem_limit_bytes=VMEM_LIMIT)


def _silu(x):
    return x * jax.nn.sigmoid(x)


def _mod_kernel(c_ref, w_ref, b_ref, o_ref):
    o_ref[0] = jnp.dot(_silu(c_ref[...]), w_ref[0], precision=HIGHEST,
                       preferred_element_type=F32) + b_ref[0]


def _modulation(c_all, w_ada, b_ada):
    n = c_all.shape[0]
    tn = D_MODEL
    return pl.pallas_call(
        _mod_kernel,
        grid=(DEPTH, 3 * D_MODEL // tn),
        in_specs=[pl.BlockSpec((n, D_MODEL), lambda l, j: (0, 0)),
                  pl.BlockSpec((1, D_MODEL, tn), lambda l, j: (l, 0, j)),
                  pl.BlockSpec((1, 1, tn), lambda l, j: (l, 0, j))],
        out_specs=pl.BlockSpec((1, n, tn), lambda l, j: (l, 0, j)),
        out_shape=jax.ShapeDtypeStruct((DEPTH, n, 3 * D_MODEL), F32),
        compiler_params=_cparams(("arbitrary", "arbitrary")),
        name="modulation",
    )(c_all, w_ada, b_ada.reshape(DEPTH, 1, 3 * D_MODEL))


def _inproj_kernel(x_ref, sc_ref, sh_ref, g_ref, w_ref, wt_ref, cost_ref, sint_ref,
                   u_ref, gate_ref, krm_ref, qt_ref, kt_ref, vt_ref):
    x = x_ref[0]
    h = x * lax.rsqrt(jnp.mean(x * x, axis=-1, keepdims=True) + EPS) * g_ref[...]
    h = h * (1.0 + sc_ref[0]) + sh_ref[0]
    hb = h.astype(BF16)

    def proj(lo, n):
        return jnp.dot(hb, w_ref[:, lo:lo + n], preferred_element_type=F32)

    u_ref[0] = proj(OFF_U, U_COLS)
    gate_ref[0] = _silu(proj(OFF_G, G_COLS)).astype(gate_ref.dtype)

    cost = cost_ref[...]
    sint = sint_ref[...]
    for n, (ref, use_rope) in enumerate(((qt_ref, True), (kt_ref, True), (vt_ref, False))):
        pt = lax.dot_general(wt_ref[n * D_ATT:(n + 1) * D_ATT, :], hb, NT_DIMS,
                             preferred_element_type=F32)
        if not use_rope:
            ref[0] = pt
            continue
        for hh in range(N_HEADS):
            r0 = hh * HEAD_DIM
            x1 = pt[r0:r0 + HALF]
            x2 = pt[r0 + HALF:r0 + HEAD_DIM]
            ref[0, r0:r0 + HALF, :] = x1 * cost - x2 * sint
            ref[0, r0 + HALF:r0 + HEAD_DIM, :] = x2 * cost + x1 * sint
    krm_ref[0] = jnp.transpose(kt_ref[0]).astype(BF16)


def _inproj(x, scale, shift, g, w_row, w_t, rope, tm):
    nb, s, _ = x.shape
    cos_t, sin_t = rope
    rm = scale.shape[1]
    mod_block = (1, tm, D_MODEL) if rm == s else (1, 1, D_MODEL)
    mod_map = (lambda b, i: (b, i, 0)) if rm == s else (lambda b, i: (b, 0, 0))
    fm_spec = pl.BlockSpec((1, D_ATT, tm), lambda b, i: (b, 0, i))
    fm_shape = jax.ShapeDtypeStruct((nb, D_ATT, s), F32)
    return pl.pallas_call(
        _inproj_kernel,
        grid=(nb, s // tm),
        in_specs=[pl.BlockSpec((1, tm, D_MODEL), lambda b, i: (b, i, 0)),
                  pl.BlockSpec(mod_block, mod_map),
                  pl.BlockSpec(mod_block, mod_map),
                  pl.BlockSpec((1, D_MODEL), lambda b, i: (0, 0)),
                  pl.BlockSpec(w_row.shape, lambda b, i: (0, 0)),
                  pl.BlockSpec(w_t.shape, lambda b, i: (0, 0)),
                  pl.BlockSpec((HALF, tm), lambda b, i: (0, i)),
                  pl.BlockSpec((HALF, tm), lambda b, i: (0, i))],
        out_specs=[pl.BlockSpec((1, tm, U_COLS), lambda b, i: (b, i, 0)),
                   pl.BlockSpec((1, tm, G_COLS), lambda b, i: (b, i, 0)),
                   pl.BlockSpec((1, tm, D_ATT), lambda b, i: (b, i, 0)),
                   fm_spec, fm_spec, fm_spec],
        out_shape=[jax.ShapeDtypeStruct((nb, s, U_COLS), F32),
                   jax.ShapeDtypeStruct((nb, s, G_COLS), BF16),
                   jax.ShapeDtypeStruct((nb, s, D_ATT), BF16),
                   fm_shape, fm_shape, fm_shape],
        compiler_params=_cparams(("arbitrary", "arbitrary")),
        name="inproj",
    )(x, scale, shift, g.reshape(1, D_MODEL), w_row, w_t, cos_t, sin_t)


def _outproj_kernel(x_ref, as_ref, att_ref, gate_ref, gm_ref, w_ref, fg_ref, o_ref, *, final):
    g = gate_ref[0].astype(F32)
    a_s = as_ref[0].astype(F32)
    mixed = jnp.concatenate([a_s[:, :D_LRU] * g[:, :D_LRU],
                             att_ref[0].astype(F32) * g[:, D_LRU:D_LRU + D_ATT],
                             a_s[:, D_LRU:] * g[:, D_LRU + D_ATT:]], axis=1).astype(BF16)
    y = jnp.dot(mixed, w_ref[...], preferred_element_type=F32)
    xn = x_ref[0] + gm_ref[0] * y
    if final:
        xn = xn * lax.rsqrt(jnp.mean(xn * xn, axis=-1, keepdims=True) + EPS) * fg_ref[...]
    o_ref[0] = xn


def _outproj(x, as_out, att, gates, gate_mod, w_out, final_g, tm, final):
    nb, s, _ = x.shape
    rm = gate_mod.shape[1]
    mod_block = (1, tm, D_MODEL) if rm == s else (1, 1, D_MODEL)
    mod_map = (lambda b, i: (b, i, 0)) if rm == s else (lambda b, i: (b, 0, 0))
    row = lambda n: pl.BlockSpec((1, tm, n), lambda b, i: (b, i, 0))
    return pl.pallas_call(
        functools.partial(_outproj_kernel, final=final),
        grid=(nb, s // tm),
        in_specs=[row(D_MODEL), row(U_COLS), row(D_ATT), row(G_COLS),
                  pl.BlockSpec(mod_block, mod_map),
                  pl.BlockSpec((D_MIX, D_MODEL), lambda b, i: (0, 0)),
                  pl.BlockSpec((1, D_MODEL), lambda b, i: (0, 0))],
        out_specs=row(D_MODEL),
        out_shape=jax.ShapeDtypeStruct((nb, s, D_MODEL), F32),
        compiler_params=_cparams(("arbitrary", "arbitrary")),
        name="outproj",
    )(x, as_out, att, gates, gate_mod, w_out, final_g.reshape(1, D_MODEL))


def _lru_gates(xc, wg_ref, bg_ref, c8_ref):
    gates = jnp.dot(xc.astype(BF16), wg_ref[...], preferred_element_type=F32) + bg_ref[...]
    r = jax.nn.sigmoid(gates[:, :D_LRU])
    i = jax.nn.sigmoid(gates[:, D_LRU:])
    log_a = r * c8_ref[...]
    a = jnp.exp(log_a)
    mult = jnp.sqrt(1.0 - jnp.exp(2.0 * log_a))
    return a, mult * i * xc


def _s5_out(x_cat_bf16, sx, cbd_ref, d_ref, wglu_ref, bglu_ref):
    y = jnp.dot(x_cat_bf16, cbd_ref[...], preferred_element_type=F32) + d_ref[...] * sx
    z = jax.nn.gelu(y)
    return z * jax.nn.sigmoid(jnp.dot(z.astype(BF16), wglu_ref[...], preferred_element_type=F32)
                              + bglu_ref[...])


def _rec_prompt_kernel(u_ref, cw_ref, cb_ref, wg_ref, bg_ref, c8_ref, lbr_ref, lbi_ref,
                       bbd_ref, cbd_ref, d_ref, wglu_ref, bglu_ref,
                       o_ref, hl_ref, cbuf_ref, sre_ref, sim_ref,
                       ext, a_s, b_s, h_s, bu_s, x_s, h_st, x_st, *, nb, tc, pitch):
    c = pl.program_id(0)
    nl = D_LRU // LANES
    ns = S5_P // LANES

    @pl.when(c == 0)
    def _():
        ext[:, 0:8, :] = jnp.zeros((nb, 8, D_LRU), F32)
        h_st[...] = jnp.zeros(h_st.shape, F32)
        x_st[...] = jnp.zeros(x_st.shape, F32)

    for b in range(nb):
        r0 = b * pitch
        ext[b, 8:8 + tc, :] = u_ref[b, :, 0:D_LRU]
        xc = cb_ref[...]
        for j in range(CONV_W):
            xc = xc + ext[b, 8 - (CONV_W - 1) + j:8 - (CONV_W - 1) + j + tc, :] * cw_ref[j:j + 1, :]
        ext[b, 8 - (CONV_W - 1):8, :] = ext[b, 8 + tc - (CONV_W - 1):8 + tc, :]
        a, bb = _lru_gates(xc, wg_ref, bg_ref, c8_ref)
        for j in range(nl):
            a_s[j, r0:r0 + tc, :] = a[:, j * LANES:(j + 1) * LANES]
            b_s[j, r0:r0 + tc, :] = bb[:, j * LANES:(j + 1) * LANES]
        sxb = u_ref[b, :, D_LRU:U_COLS].astype(BF16)
        for j in range(ns):
            bu = jnp.dot(sxb, bbd_ref[:, 2 * j * LANES:(2 * j + 2) * LANES], preferred_element_type=F32)
            bu_s[j, r0:r0 + tc, :] = bu[:, :LANES]
            bu_s[ns + j, r0:r0 + tc, :] = bu[:, LANES:]

    def step(t, carry):
        h, xr, xi = carry
        rows = pl.ds(t, nb, stride=pitch)
        h_new, xr_new, xi_new = [], [], []
        for j in range(nl):
            hj = a_s[j, rows, :] * h[j] + b_s[j, rows, :]
            h_s[j, rows, :] = hj
            h_new.append(hj)
        for j in range(ns):
            lr = lbr_ref[j]
            li = lbi_ref[j]
            nr = lr * xr[j] - li * xi[j] + bu_s[j, rows, :]
            ni = lr * xi[j] + li * xr[j] + bu_s[ns + j, rows, :]
            x_s[j, rows, :] = nr
            x_s[ns + j, rows, :] = ni
            xr_new.append(nr)
            xi_new.append(ni)
        return tuple(h_new), tuple(xr_new), tuple(xi_new)

    init = (tuple(h_st[j] for j in range(nl)),
            tuple(x_st[j] for j in range(ns)),
            tuple(x_st[ns + j] for j in range(ns)))
    h, xr, xi = lax.fori_loop(0, tc, step, init)
    for j in range(nl):
        h_st[j] = h[j]
        hl_ref[:, j * LANES:(j + 1) * LANES] = h[j]
    for j in range(ns):
        x_st[j] = xr[j]
        x_st[ns + j] = xi[j]
        sre_ref[:, j * LANES:(j + 1) * LANES] = xr[j]
        sim_ref[:, j * LANES:(j + 1) * LANES] = xi[j]

    for b in range(nb):
        r0 = b * pitch
        for j in range(nl):
            o_ref[b, :, j * LANES:(j + 1) * LANES] = h_s[j, r0:r0 + tc, :].astype(o_ref.dtype)
        x_cat = jnp.concatenate([x_s[j, r0:r0 + tc, :].astype(BF16) for j in range(2 * ns)], axis=1)
        o_ref[b, :, D_LRU:U_COLS] = _s5_out(x_cat, u_ref[b, :, D_LRU:U_COLS], cbd_ref, d_ref,
                                            wglu_ref, bglu_ref).astype(o_ref.dtype)
        cbuf_ref[b] = ext[b, 8 - (CONV_W - 1):8, :]


def _rec_prompt(u, pr, tc):
    nb, s, _ = u.shape
    pitch = tc + 8
    nl = D_LRU // LANES
    ns = S5_P // LANES
    full = lambda a: pl.BlockSpec(a.shape, lambda c, _n=a.ndim: (0,) * _n)
    params = [pr["cw"], pr["cb"], pr["wg"], pr["bg"], pr["c8"], pr["lbr"], pr["lbi"],
              pr["bbd"], pr["cbd"], pr["d"], pr["wglu"], pr["bglu"]]
    st = lambda *shape: pl.BlockSpec(shape, lambda c, _n=len(shape): (0,) * _n)
    return pl.pallas_call(
        functools.partial(_rec_prompt_kernel, nb=nb, tc=tc, pitch=pitch),
        grid=(s // tc,),
        in_specs=[pl.BlockSpec((nb, tc, U_COLS), lambda c: (0, c, 0))] + [full(a) for a in params],
        out_specs=[pl.BlockSpec((nb, tc, U_COLS), lambda c: (0, c, 0)),
                   st(nb, D_LRU), st(nb, CONV_W - 1, D_LRU), st(nb, S5_P), st(nb, S5_P)],
        out_shape=[jax.ShapeDtypeStruct((nb, s, U_COLS), BF16),
                   jax.ShapeDtypeStruct((nb, D_LRU), F32),
                   jax.ShapeDtypeStruct((nb, CONV_W - 1, D_LRU), F32),
                   jax.ShapeDtypeStruct((nb, S5_P), F32),
                   jax.ShapeDtypeStruct((nb, S5_P), F32)],
        scratch_shapes=[pltpu.VMEM((nb, tc + 8, D_LRU), F32),
                        pltpu.VMEM((nl, nb * pitch, LANES), F32),
                        pltpu.VMEM((nl, nb * pitch, LANES), F32),
                        pltpu.VMEM((nl, nb * pitch, LANES), F32),
                        pltpu.VMEM((2 * ns, nb * pitch, LANES), F32),
                        pltpu.VMEM((2 * ns, nb * pitch, LANES), F32),
                        pltpu.VMEM((nl, nb, LANES), F32),
                        pltpu.VMEM((2 * ns, nb, LANES), F32)],
        compiler_params=_cparams(("arbitrary",)),
        name="rec_prompt",
    )(u, *params)


def _rec_sample_kernel(u_ref, h0_ref, buf0_ref, s0r_ref, s0i_ref,
                       cw_ref, cb_ref, wg_ref, bg_ref, c8_ref, lbr_ref, lbi_ref,
                       bbd_ref, cbd_ref, d_ref, wglu_ref, bglu_ref,
                       o_ref, hl_ref, cbuf_ref, sre_ref, sim_ref, *, nt):
    ns = S5_P // LANES
    ext = [buf0_ref[j] for j in range(CONV_W - 1)] + [u_ref[t, :, 0:D_LRU] for t in range(nt)]
    xcs = []
    for t in range(nt):
        xc = cb_ref[...]
        for j in range(CONV_W):
            xc = xc + ext[t + j] * cw_ref[j:j + 1, :]
        xcs.append(xc)
    a, bb = _lru_gates(jnp.concatenate(xcs, axis=0), wg_ref, bg_ref, c8_ref)
    nbat = h0_ref.shape[0]
    h = h0_ref[...]
    for t in range(nt):
        h = a[t * nbat:(t + 1) * nbat] * h + bb[t * nbat:(t + 1) * nbat]
        o_ref[t, :, 0:D_LRU] = h.astype(o_ref.dtype)
    hl_ref[...] = h
    for j in range(CONV_W - 1):
        cbuf_ref[j] = ext[nt + j]

    sx = jnp.concatenate([u_ref[t, :, D_LRU:U_COLS] for t in range(nt)], axis=0)
    bu = jnp.dot(sx.astype(BF16), bbd_ref[...], preferred_element_type=F32)
    lr = jnp.concatenate([lbr_ref[j] for j in range(ns)], axis=1)
    li = jnp.concatenate([lbi_ref[j] for j in range(ns)], axis=1)
    xr = s0r_ref[...]
    xi = s0i_ref[...]
    xs = []
    for t in range(nt):
        but = bu[t * nbat:(t + 1) * nbat]
        bur = jnp.concatenate([but[:, 2 * j * LANES:(2 * j + 1) * LANES] for j in range(ns)], axis=1)
        bui = jnp.concatenate([but[:, (2 * j + 1) * LANES:(2 * j + 2) * LANES] for j in range(ns)], axis=1)
        xr, xi = lr * xr - li * xi + bur, lr * xi + li * xr + bui
        xs.append(jnp.concatenate([xr, xi], axis=1).astype(BF16))
    sre_ref[...] = xr
    sim_ref[...] = xi
    s_out = _s5_out(jnp.concatenate(xs, axis=0), sx, cbd_ref, d_ref, wglu_ref, bglu_ref)
    for t in range(nt):
        o_ref[t, :, D_LRU:U_COLS] = s_out[t * nbat:(t + 1) * nbat].astype(o_ref.dtype)


def _rec_sample(u_t, h0, buf0_t, s0r, s0i, pr):
    nt, nbat, _ = u_t.shape
    params = [pr["cw"], pr["cb"], pr["wg"], pr["bg"], pr["c8"], pr["lbr"], pr["lbi"],
              pr["bbd"], pr["cbd"], pr["d"], pr["wglu"], pr["bglu"]]
    return pl.pallas_call(
        functools.partial(_rec_sample_kernel, nt=nt),
        out_shape=[jax.ShapeDtypeStruct((nt, nbat, U_COLS), BF16),
                   jax.ShapeDtypeStruct((nbat, D_LRU), F32),
                   jax.ShapeDtypeStruct((CONV_W - 1, nbat, D_LRU), F32),
                   jax.ShapeDtypeStruct((nbat, S5_P), F32),
                   jax.ShapeDtypeStruct((nbat, S5_P), F32)],
        compiler_params=pltpu.CompilerParams(vmem_limit_bytes=VMEM_LIMIT),
        name="rec_sample",
    )(u_t, h0, buf0_t, s0r, s0i, *params)


def _rec_params(lru_conv_w, lru_conv_b, lru_w_r, lru_b_r, lru_w_i, lru_b_i, lru_lam,
                s5_a_re, s5_a_im, s5_log_dt, s5_b_re, s5_b_im, s5_c_re, s5_c_im, s5_d, s5_w_glu, s5_b_glu):
    ns = S5_P // LANES
    def bd(w):
        n, r, c = w.shape
        on_diag = jnp.eye(n, dtype=bool)[:, None, :, None]
        return jnp.where(on_diag, w[:, :, None, :], 0.0).reshape(n * r, n * c)
    wg = jnp.concatenate([bd(lru_w_r), bd(lru_w_i)], axis=1).astype(BF16)
    bg = jnp.concatenate([lru_b_r, lru_b_i]).reshape(1, 2 * D_LRU)
    c8 = (LRU_C * jax.nn.log_sigmoid(lru_lam.astype(F32))).reshape(1, D_LRU)
    dt = jnp.exp(s5_log_dt.astype(F32))[:, None]
    ar = s5_a_re.astype(F32)
    ai = s5_a_im.astype(F32)
    mag = jnp.exp(ar * dt)
    lb_re = mag * jnp.cos(ai * dt)
    lb_im = mag * jnp.sin(ai * dt)
    den = ar * ar + ai * ai
    n_re = lb_re - 1.0
    co_re = (n_re * ar + lb_im * ai) / den
    co_im = (lb_im * ar - n_re * ai) / den
    bb_re = co_re[..., None] * s5_b_re - co_im[..., None] * s5_b_im
    bb_im = co_re[..., None] * s5_b_im + co_im[..., None] * s5_b_re
    bre = bd(jnp.swapaxes(bb_re, 1, 2))
    bim = bd(jnp.swapaxes(bb_im, 1, 2))
    bbd = jnp.stack([bre.reshape(D_S5, ns, LANES), bim.reshape(D_S5, ns, LANES)], axis=2)
    bbd = bbd.reshape(D_S5, 2 * S5_P).astype(BF16)
    cre = bd(jnp.swapaxes(s5_c_re, 1, 2))
    cim = bd(jnp.swapaxes(s5_c_im, 1, 2))
    cbd = jnp.concatenate([cre, -cim], axis=0).astype(BF16)
    return dict(cw=lru_conv_w, cb=lru_conv_b.reshape(1, D_LRU), wg=wg, bg=bg, c8=c8,
                lbr=lb_re.reshape(ns, 1, LANES), lbi=lb_im.reshape(ns, 1, LANES),
                bbd=bbd, cbd=cbd, d=s5_d.reshape(1, D_S5),
                wglu=s5_w_glu.astype(BF16), bglu=s5_b_glu.reshape(1, D_S5))


def _moba_prompt_kernel(pt_ref, qt_ref, krm_ref, kt_ref, vt_ref, cache_ref, o_ref, ks_ref,
                        km_s, vb_s, bias_s, sc_s, pbuf, psem, *, s, nblk, grp, nh, layer, pg, nsteps):
    qi = pl.program_id(2)
    step = (pl.program_id(0) * (N_HEADS // nh) + pl.program_id(1)) * nblk + qi
    tq = MOBA_BLOCK
    hd = HEAD_DIM
    gk = grp * tq

    @pl.when(qi == 0)
    def _():
        r = lax.broadcasted_iota(jnp.int32, (nblk, s), 1) // MOBA_BLOCK
        n = lax.broadcasted_iota(jnp.int32, (nblk, s), 0)
        pm = jnp.where(r == n, 1.0 / MOBA_BLOCK, 0.0).astype(F32)
        for hh in range(nh):
            km_s[hh] = lax.dot_general(pm, kt_ref[0, hh * hd:(hh + 1) * hd, :], NT_DIMS,
                                       precision=HIGHEST, preferred_element_type=F32)
        ones_row = (lax.broadcasted_iota(jnp.int32, (VB_ROWS - hd, gk), 0) == 0).astype(BF16)
        for g in range(nblk // grp):
            for hh in range(nh):
                vb_s[g, hh, 0:hd, :] = vt_ref[0, hh * hd:(hh + 1) * hd, g * gk:(g + 1) * gk].astype(BF16)
                vb_s[g, hh, hd:VB_ROWS, :] = ones_row

    qt = qt_ref[0]
    qs = (qt * (hd ** -0.5 * LOG2E)).astype(BF16)
    feat = lax.broadcasted_iota(jnp.int32, (nh * hd, tq), 0)
    blk_iota = lax.broadcasted_iota(jnp.int32, (nblk, tq), 0)
    past = blk_iota < qi
    ws = []
    for hh in range(nh):
        ws.append(jnp.where((feat >= hh * hd) & (feat < (hh + 1) * hd), qs, jnp.zeros_like(qs)))
        sb = jnp.dot(km_s[hh], qt[hh * hd:(hh + 1) * hd, :], precision=HIGHEST,
                     preferred_element_type=F32)
        sb = jnp.where(past, sb, -jnp.inf)
        cnt = jnp.zeros((nblk, tq), jnp.int32)
        for i in range(nblk):
            ri = sb[i:i + 1, :]
            beats = (ri > sb) | ((ri == sb) & (i < blk_iota))
            cnt = cnt + beats.astype(jnp.int32)
        sel = (cnt < MOBA_TOPK) & past
        bias_s[hh] = jnp.where(sel | (blk_iota == qi), 0.0, NEG).astype(F32)

    def scores(g, hh):
        st = pl.multiple_of(g * gk, gk)
        sc = jnp.dot(krm_ref[0, pl.ds(st, gk), :], ws[hh], preferred_element_type=F32)
        return jnp.concatenate(
            [sc[i * tq:(i + 1) * tq] + bias_s[hh, pl.ds(g * grp + i, 1), :] for i in range(grp)], axis=0)

    def pass1(g, ms):
        st = pl.multiple_of(g * gk, gk)
        out = []
        for hh in range(nh):
            sc = scores(g, hh)
            sc_s[hh, pl.ds(st, gk), :] = sc
            out.append(jnp.maximum(ms[hh], jnp.max(sc, axis=0, keepdims=True)))
        return tuple(out)

    g_own = qi // grp
    ms = lax.fori_loop(0, g_own, pass1, tuple(jnp.full((1, tq), NEG, F32) for _ in range(nh)))

    _ksum_fetch(pt_ref, cache_ref, ks_ref, pbuf, psem, step, layer=layer, nsteps=nsteps, pg=pg)
    half = N_HEADS // 2

    r_own = qi % grp
    st_own = pl.multiple_of(g_own * gk, gk)
    causal = (lax.broadcasted_iota(jnp.int32, (tq, tq), 0) <= lax.broadcasted_iota(jnp.int32, (tq, tq), 1))

    def own_pass1(nb_own):
        rows = nb_own * tq

        def run(ms):
            _ksum_merge(ks_ref, pbuf, step, range(0, half), pg=pg)
            out = []
            for hh in range(nh):
                sc = jnp.dot(krm_ref[0, pl.ds(st_own, rows), :], ws[hh], preferred_element_type=F32)
                parts = [sc[i * tq:(i + 1) * tq] + bias_s[hh, pl.ds(g_own * grp + i, 1), :]
                         for i in range(nb_own - 1)]
                parts.append(jnp.where(causal, sc[rows - tq:rows], NEG))
                sc = jnp.concatenate(parts, axis=0)
                sc_s[hh, pl.ds(st_own, rows), :] = sc
                out.append(jnp.maximum(ms[hh], jnp.max(sc, axis=0, keepdims=True)))
            return tuple(out)
        return run

    ms = lax.switch(r_own, [own_pass1(i + 1) for i in range(grp)], ms)

    def pass2_group(g, rows, accs):
        st = pl.multiple_of(g * gk, gk)
        out = []
        for hh in range(nh):
            p = jnp.exp2(sc_s[hh, pl.ds(st, rows), :] - ms[hh])
            out.append(accs[hh] + jnp.dot(vb_s[g, hh, :, 0:rows], p.astype(BF16), preferred_element_type=F32))
        return tuple(out)

    accs = lax.fori_loop(0, g_own, lambda g, a: pass2_group(g, gk, a),
                         tuple(jnp.zeros((VB_ROWS, tq), F32) for _ in range(nh)))
    def own_pass2(nb_own):
        def run(accs):
            _ksum_merge(ks_ref, pbuf, step, range(half, N_HEADS), pg=pg)
            return pass2_group(g_own, nb_own * tq, accs)
        return run

    accs = lax.switch(r_own, [own_pass2(i + 1) for i in range(grp)], accs)
    out_t = jnp.concatenate([acc[0:hd] / acc[hd:hd + 1] for acc in accs], axis=0)
    o_ref[0] = jnp.transpose(out_t).astype(o_ref.dtype)


def _moba_prompt(qt, krm, kt, vt, pt_flat, cache_kt, layer):
    nb, _, s = qt.shape
    nblk = s // MOBA_BLOCK
    tq = MOBA_BLOCK
    nh = 4
    grp = 4
    hp = nh * HEAD_DIM
    nhp = N_HEADS // nh
    nsteps = nb * nhp * nblk
    pg = pt_flat.shape[0] // nsteps
    nbat = pt_flat.shape[0] // N_PAGES
    steps_per_b = N_PAGES // pg
    once = dict(pipeline_mode=pl.Buffered(1))
    return pl.pallas_call(
        functools.partial(_moba_prompt_kernel, s=s, nblk=nblk, grp=grp, nh=nh, layer=layer, pg=pg,
                          nsteps=nsteps),
        grid_spec=pltpu.PrefetchScalarGridSpec(
            num_scalar_prefetch=1,
            grid=(nb, nhp, nblk),
            in_specs=[pl.BlockSpec((1, hp, tq), lambda b, h, i, pt: (b, h, i)),
                      pl.BlockSpec((1, s, hp), lambda b, h, i, pt: (b, 0, h), **once),
                      pl.BlockSpec((1, hp, s), lambda b, h, i, pt: (b, h, 0), **once),
                      pl.BlockSpec((1, hp, s), lambda b, h, i, pt: (b, h, 0), **once),
                      pl.BlockSpec(memory_space=pl.ANY)],
            out_specs=[pl.BlockSpec((1, tq, hp), lambda b, h, i, pt: (b, i, h)),
                       pl.BlockSpec((1, N_HEADS, HEAD_DIM, LANES),
                                    lambda b, h, i, pt: (((b * nhp + h) * nblk + i) // steps_per_b, 0, 0, 0))],
            scratch_shapes=[pltpu.VMEM((nh, nblk, HEAD_DIM), F32),
                            pltpu.VMEM((nblk // grp, nh, VB_ROWS, grp * tq), BF16),
                            pltpu.VMEM((nh, nblk, tq), F32),
                            pltpu.VMEM((nh, s, tq), F32),
                            pltpu.VMEM((2, pg, N_HEADS, HEAD_DIM, PAGE_SIZE), F32),
                            pltpu.SemaphoreType.DMA((2, pg))]),
        out_shape=[jax.ShapeDtypeStruct((nb, s, D_ATT), BF16),
                   jax.ShapeDtypeStruct((nbat, N_HEADS, HEAD_DIM, LANES), F32)],
        compiler_params=_cparams(("arbitrary", "arbitrary", "arbitrary")),
        name="moba_prompt",
    )(pt_flat, qt, krm, kt, vt, cache_kt)


PAGES_PER_BLOCK = MOBA_BLOCK // PAGE_SIZE
N_PAGES = PAST_LEN // PAGE_SIZE
N_FULL = PAST_LEN // MOBA_BLOCK
SAMPLE_SLOTS = 3


def _ksum_fetch(pt_ref, cache_ref, o_ref, buf, sem, n, *, layer, nsteps, pg):
    slot = n % 2
    g = n % (N_PAGES // pg)

    def start(step, sl):
        for j in range(pg):
            pid = pt_ref[step * pg + j]
            pltpu.make_async_copy(cache_ref.at[layer, pid], buf.at[sl, j], sem.at[sl, j]).start()

    @pl.when(n == 0)
    def _():
        start(0, 0)

    @pl.when(n + 1 < nsteps)
    def _():
        start(n + 1, 1 - slot)

    @pl.when(g == 0)
    def _():
        o_ref[...] = jnp.zeros(o_ref.shape, F32)

    for j in range(pg):
        pltpu.make_async_copy(buf.at[slot, j], buf.at[slot, j], sem.at[slot, j]).wait()


def _ksum_merge(o_ref, buf, n, heads, *, pg):
    slot = n % 2
    g = n % (N_PAGES // pg)
    bpg = pg // PAGES_PER_BLOCK
    lane = lax.broadcasted_iota(jnp.int32, (HEAD_DIM, LANES), 1)
    for hh in heads:
        acc = o_ref[0, hh]
        for blk in range(bpg):
            tile = buf[slot, PAGES_PER_BLOCK * blk, hh]
            for pp in range(1, PAGES_PER_BLOCK):
                tile = tile + buf[slot, PAGES_PER_BLOCK * blk + pp, hh]
            acc = jnp.where(lane == g * bpg + blk, jnp.sum(tile, axis=1, keepdims=True), acc)
        o_ref[0, hh] = acc


def _topk_kernel(q_ref, ks_ref, o_ref):
    nq = q_ref.shape[2]
    lane = lax.broadcasted_iota(jnp.int32, (nq, LANES), 1)
    for hh in range(N_HEADS):
        km = ks_ref[0, hh] * (1.0 / MOBA_BLOCK)
        sb = jnp.dot(q_ref[0, hh], km, precision=HIGHEST, preferred_element_type=F32)
        sb = jnp.where(lane < N_FULL, sb, -jnp.inf)
        out = jnp.zeros((nq, LANES), jnp.int32)
        for r in range(MOBA_TOPK):
            mx = jnp.max(sb, axis=1, keepdims=True)
            idx = jnp.min(jnp.where(sb == mx, lane, LANES), axis=1, keepdims=True)
            out = jnp.where(lane == r, idx, out)
            sb = jnp.where(lane == idx, -jnp.inf, sb)
        o_ref[0, hh] = out


def _sample_topk(q_pad, ksum_t):
    nbat, nh, nq, hd = q_pad.shape
    return pl.pallas_call(
        _topk_kernel,
        grid=(nbat,),
        in_specs=[pl.BlockSpec((1, nh, nq, hd), lambda b: (b, 0, 0, 0)),
                  pl.BlockSpec((1, nh, hd, LANES), lambda b: (b, 0, 0, 0))],
        out_specs=pl.BlockSpec((1, nh, nq, LANES), lambda b: (b, 0, 0, 0)),
        out_shape=jax.ShapeDtypeStruct((nbat, nh, nq, LANES), jnp.int32),
        compiler_params=_cparams(("arbitrary",)),
        name="sample_topk",
    )(q_pad, ksum_t)


def _slab_copy(src, buf, sem, slot, j):
    return pltpu.make_async_copy(src, buf.at[slot, j], sem.at[slot, j])


def _moba_sample_kernel(idx_ref, pt_ref, q_ref, kn_ref, vn_ref, ck_ref, cv_ref, o_ref,
                        kbuf, vbuf, ksem, vsem, *, layer, nsteps, nt, hps):
    n = pl.program_id(0)
    slot = n % SAMPLE_SLOTS
    b = n // (N_HEADS // hps)
    nsel = MOBA_TOPK * PAGES_PER_BLOCK
    nslab = nt * nsel
    ahead = SAMPLE_SLOTS - 1

    def start(step, sl):
        sb = step // (N_HEADS // hps)
        for e in range(hps):
            bh = step * hps + e
            sh = bh % N_HEADS
            for t in range(nt):
                for r in range(MOBA_TOPK):
                    blk = idx_ref[(bh * nt + t) * MOBA_TOPK + r]
                    for pp in range(PAGES_PER_BLOCK):
                        pid = pt_ref[sb * N_PAGES + blk * PAGES_PER_BLOCK + pp]
                        j = e * nslab + (t * MOBA_TOPK + r) * PAGES_PER_BLOCK + pp
                        _slab_copy(ck_ref.at[layer, pid, sh], kbuf, ksem, sl, j).start()
                        _slab_copy(cv_ref.at[layer, pid, sh], vbuf, vsem, sl, j).start()

    def wait(sl):
        for j in range(hps * nslab):
            _slab_copy(kbuf.at[sl, j], kbuf, ksem, sl, j).wait()
            _slab_copy(vbuf.at[sl, j], vbuf, vsem, sl, j).wait()

    @pl.when(n == 0)
    def _():
        for s0 in range(ahead):
            start(s0, s0)

    @pl.when(n + ahead < nsteps)
    def _():
        start(n + ahead, (n + ahead) % SAMPLE_SLOTS)

    wait(slot)

    nq = q_ref.shape[2]
    ntok = kn_ref.shape[2]
    nkeys = nsel * PAGE_SIZE
    qrow = lax.broadcasted_iota(jnp.int32, (nq, nkeys), 0)
    orow = lax.broadcasted_iota(jnp.int32, (nq, ntok), 0)
    opos = lax.broadcasted_iota(jnp.int32, (nq, ntok), 1) - b * nt
    own_ok = (opos >= 0) & (opos <= orow)
    for e in range(hps):
        qs = (q_ref[0, e] * (HEAD_DIM ** -0.5)).astype(BF16)
        base = e * nslab
        scores = []
        for t in range(nt):
            kt = jnp.concatenate([kbuf[slot, base + t * nsel + i] for i in range(nsel)], axis=1).astype(BF16)
            st = jnp.dot(qs, kt, preferred_element_type=F32)
            scores.append(jnp.where(qrow == t, st, -jnp.inf))
        so = jnp.dot(qs, kn_ref[e].astype(BF16), preferred_element_type=F32)
        so = jnp.where(own_ok, so, -jnp.inf)
        m = jnp.max(so, axis=1, keepdims=True)
        for st in scores:
            m = jnp.maximum(m, jnp.max(st, axis=1, keepdims=True))
        po = jnp.exp(so - m)
        den = jnp.sum(po, axis=1, keepdims=True)
        acc = lax.dot_general(po.astype(BF16), vn_ref[e].astype(BF16), NT_DIMS, preferred_element_type=F32)
        for t in range(nt):
            pt_ = jnp.exp(scores[t] - m)
            den = den + jnp.sum(pt_, axis=1, keepdims=True)
            vt = jnp.concatenate([vbuf[slot, base + t * nsel + i] for i in range(nsel)], axis=1).astype(BF16)
            acc = acc + lax.dot_general(pt_.astype(BF16), vt, NT_DIMS, preferred_element_type=F32)
        o_ref[0, e] = acc / den


def _moba_sample(q_pad, kt_new, vt_new, cache_kt, cache_vt, idx_flat, pt_flat, layer, nt):
    nbat, nh, nq, hd = q_pad.shape
    ntok = kt_new.shape[1]
    hps = 4
    npair = nh // hps
    nsteps = nbat * npair
    nslab = hps * nt * MOBA_TOPK * PAGES_PER_BLOCK
    qtok = pl.BlockSpec((1, hps, nq, hd), lambda n, idx, pt: (n // npair, n % npair, 0, 0))
    new = pl.BlockSpec((hps, hd, ntok), lambda n, idx, pt: (n % npair, 0, 0))
    return pl.pallas_call(
        functools.partial(_moba_sample_kernel, layer=layer, nsteps=nsteps, nt=nt, hps=hps),
        grid_spec=pltpu.PrefetchScalarGridSpec(
            num_scalar_prefetch=2,
            grid=(nsteps,),
            in_specs=[qtok, new, new, pl.BlockSpec(memory_space=pl.ANY), pl.BlockSpec(memory_space=pl.ANY)],
            out_specs=qtok,
            scratch_shapes=[pltpu.VMEM((SAMPLE_SLOTS, nslab, hd, PAGE_SIZE), F32),
                            pltpu.VMEM((SAMPLE_SLOTS, nslab, hd, PAGE_SIZE), F32),
                            pltpu.SemaphoreType.DMA((SAMPLE_SLOTS, nslab)),
                            pltpu.SemaphoreType.DMA((SAMPLE_SLOTS, nslab))]),
        out_shape=jax.ShapeDtypeStruct((nbat, nh, nq, hd), F32),
        compiler_params=_cparams(("arbitrary",)),
        name="moba_sample",
    )(idx_flat, pt_flat, q_pad, kt_new.reshape(nh, hd, ntok), vt_new.reshape(nh, hd, ntok),
      cache_kt, cache_vt)


def _rope_tables(pos):
    inv = jnp.power(ROPE_THETA, -jnp.arange(HALF, dtype=F32) / HALF)
    ang = inv[:, None] * pos.astype(F32)[None, :]
    return jnp.cos(ang), jnp.sin(ang)


def _split_w_in(w):
    a_x, a_g, q, k, v, b_g, s_x, s_g = jnp.split(
        w, (D_LRU, 2 * D_LRU, 2 * D_LRU + D_ATT, 2 * D_LRU + 2 * D_ATT, 2 * D_LRU + 3 * D_ATT,
            2 * D_LRU + 4 * D_ATT, 2 * D_LRU + 4 * D_ATT + D_S5), axis=1)
    w_row = jnp.concatenate([a_x, s_x, a_g, b_g, s_g], axis=1).astype(BF16)
    w_t = jnp.concatenate([q, k, v], axis=1).T.astype(BF16)
    return w_row, w_t


def kernel(x_prompt, x_sample, cache_k, cache_v, state_lru_h, state_lru_conv, state_s5_re, state_s5_im,
           page_table, c_prompt, c_sample, norm_g, w_ada, b_ada, w_in, w_out, lru_conv_w, lru_conv_b,
           lru_w_r, lru_b_r, lru_w_i, lru_b_i, lru_lam, s5_a_re, s5_a_im, s5_log_dt, s5_b_re, s5_b_im,
           s5_c_re, s5_c_im, s5_d, s5_w_glu, s5_b_glu, final_g):
    nbp, seq, _ = x_prompt.shape
    nbs, nts, _ = x_sample.shape
    ntok_s = nbs * nts

    rope_p = _rope_tables(jnp.arange(seq, dtype=jnp.int32))
    rope_s = _rope_tables(PAST_LEN + (jnp.arange(ntok_s, dtype=jnp.int32) % nts))

    n_c = nbp + nbs
    c_all = jnp.concatenate([c_prompt, c_sample, jnp.zeros((-n_c % 8, D_MODEL), F32)], axis=0)
    mod = _modulation(c_all, w_ada, b_ada)

    cache_kt = jnp.swapaxes(cache_k, 3, 4)
    cache_vt = jnp.swapaxes(cache_v, 3, 4)
    pt_flat = page_table.reshape(-1)
    xp = x_prompt
    xs = x_sample.reshape(1, ntok_s, D_MODEL)
    outs_p = [[] for _ in range(6)]
    outs_s = [[] for _ in range(6)]
    for l in range(DEPTH):
        last = l == DEPTH - 1
        w_row, w_t = _split_w_in(w_in[l])
        w_o = w_out[l].astype(BF16)
        pr = _rec_params(lru_conv_w[l], lru_conv_b[l], lru_w_r[l], lru_b_r[l], lru_w_i[l], lru_b_i[l],
                         lru_lam[l], s5_a_re[l], s5_a_im[l], s5_log_dt[l], s5_b_re[l], s5_b_im[l],
                         s5_c_re[l], s5_c_im[l], s5_d[l], s5_w_glu[l], s5_b_glu[l])
        shift, scale, gate = jnp.split(mod[l], 3, axis=-1)

        mp = lambda a: a[:nbp].reshape(nbp, 1, D_MODEL)
        u, gates, krm, qt, kt, vt = _inproj(xp, mp(scale), mp(shift), norm_g[l], w_row, w_t, rope_p, tm=512)
        as_out, h_last, cbuf, s_re, s_im = _rec_prompt(u, pr, tc=256)
        att, ksum_t = _moba_prompt(qt, krm, kt, vt, pt_flat, cache_kt, l)
        xp = _outproj(xp, as_out, att, gates, mp(gate), w_o, final_g, tm=512, final=last)
        for lst, val in zip(outs_p, (kt.reshape(nbp, N_HEADS, HEAD_DIM, seq), vt.reshape(nbp, N_HEADS, HEAD_DIM, seq),
                                     h_last, cbuf, s_re.reshape(nbp, S5_GROUPS, S5_STATE),
                                     s_im.reshape(nbp, S5_GROUPS, S5_STATE))):
            lst.append(val)

        ms = lambda a: jnp.repeat(a[nbp:n_c], nts, axis=0).reshape(1, ntok_s, D_MODEL)
        u, gates, _, qt, kt, vt = _inproj(xs, ms(scale), ms(shift), norm_g[l], w_row, w_t, rope_s, tm=ntok_s)
        to_heads = lambda a: a.reshape(N_HEADS, HEAD_DIM, nbs, nts).transpose(2, 0, 3, 1)
        q, k, v = to_heads(qt), to_heads(kt), to_heads(vt)
        u_t = u.reshape(nbs, nts, U_COLS).transpose(1, 0, 2)
        as_t, h_last, cbuf_t, s_re, s_im = _rec_sample(
            u_t, state_lru_h[l], state_lru_conv[l].transpose(1, 0, 2),
            state_s5_re[l].reshape(nbs, S5_P), state_s5_im[l].reshape(nbs, S5_P), pr)
        as_out = as_t.transpose(1, 0, 2).reshape(1, ntok_s, U_COLS)
        q_pad = jnp.pad(q, ((0, 0), (0, 0), (0, -nts % 8), (0, 0)))
        idx = _sample_topk(q_pad, ksum_t)[:, :, :nts, :MOBA_TOPK].reshape(-1)
        att = _moba_sample(q_pad, kt[0], vt[0], cache_kt, cache_vt, idx, pt_flat, l, nts)[:, :, :nts]
        att = att.transpose(0, 2, 1, 3).reshape(1, ntok_s, D_ATT).astype(BF16)
        xs = _outproj(xs, as_out, att, gates, ms(gate), w_o, final_g, tm=ntok_s, final=last)
        for lst, val in zip(outs_s, (k, v, h_last, cbuf_t.transpose(1, 0, 2),
                                     s_re.reshape(nbs, S5_GROUPS, S5_STATE),
                                     s_im.reshape(nbs, S5_GROUPS, S5_STATE))):
            lst.append(val)

    k_p, v_p = (jnp.swapaxes(jnp.stack(o), 3, 4) for o in outs_p[:2])
    return (xp, xs.reshape(nbs, nts, D_MODEL), k_p, v_p,
            *[jnp.stack(o) for o in outs_p[2:]], *[jnp.stack(o) for o in outs_s])
```
